```python
import jax, jax.numpy as jnp
from jax import lax
import numpy as np

D_MODEL = 1024
BATCH = 2
SEQ = 16384
DEPTH = 1
DEC_BATCH = 1
DEC_SEQ = 16384
PAST_LEN = 128

MLA_HEADS = 8
MLA_NOPE_DIM = 64
MLA_ROPE_DIM = 32
MLA_V_DIM = 64
Q_LORA_RANK = 384
KV_LORA_RANK = 256
D_ATTN = MLA_HEADS * MLA_V_DIM
MLSTM_HEADS = 4
MLSTM_HEAD_DIM = 128
D_MLSTM = MLSTM_HEADS * MLSTM_HEAD_DIM
D_MIX = D_ATTN + D_MLSTM
MLSTM_CHUNK = 128
Q_BLOCK = 128
CONV_WIDTH = 3
D_FF = 2816
ROPE_THETA = 10000.0
EPS = 1e-6
N_IN = Q_LORA_RANK + KV_LORA_RANK + MLA_ROPE_DIM + 4 * D_MLSTM + 4 * MLSTM_HEADS
IN_SPLITS = tuple(np.cumsum([Q_LORA_RANK, KV_LORA_RANK, MLA_ROPE_DIM, 2 * D_MLSTM, D_MLSTM, D_MLSTM]).tolist())

kernel_name = 'hymba_mla_mlstm_adaln_encoder'


def rmsnorm(x, g):
    xf = x.astype(jnp.float32)
    y = xf * lax.rsqrt(jnp.mean(xf * xf, axis=-1, keepdims=True) + EPS)
    return (y * g.astype(jnp.float32)).astype(x.dtype)


def modulate(h, shift, scale):
    return h * (1 + scale[:, None, :]) + shift[:, None, :]


def dwconv(x, w, b):
    C = x.shape[-1]
    pad = CONV_WIDTH // 2
    y = lax.conv_general_dilated(x, w[:, None, :].astype(x.dtype), window_strides=(1,),
                                 padding=((pad, pad),), dimension_numbers=('NWC', 'WIO', 'NWC'),
                                 feature_group_count=C)
    return y + b.astype(x.dtype)


def rope(x, pos):
    half = x.shape[-1] // 2
    inv = 1.0 / (ROPE_THETA ** (jnp.arange(half, dtype=jnp.float32) * (2.0 / x.shape[-1])))
    ang = pos.astype(jnp.float32)[:, None] * inv[None, :]
    cos = jnp.cos(ang)[None, :, None, :].astype(x.dtype)
    sin = jnp.sin(ang)[None, :, None, :].astype(x.dtype)
    x1, x2 = x[..., :half], x[..., half:]
    return jnp.concatenate([x1 * cos - x2 * sin, x2 * cos + x1 * sin], axis=-1)


def attend_blocks(q, k, v):
    B, S, H, DQK = q.shape
    DV = v.shape[-1]
    nb = S // Q_BLOCK
    scale = DQK ** -0.5
    qb = q.reshape(B, nb, Q_BLOCK, H, DQK).transpose(1, 0, 2, 3, 4)

    def one_block(qi):
        s = jnp.einsum('bqhd,bkhd->bhqk', qi, k, preferred_element_type=jnp.float32) * scale
        p = jax.nn.softmax(s, axis=-1).astype(v.dtype)
        return jnp.einsum('bhqk,bkhd->bqhd', p, v)

    o = lax.map(one_block, qb)
    return o.transpose(1, 0, 2, 3, 4).reshape(B, S, H * DV)


def mla_branch(cq, ckv, kr, q_norm_g, kv_norm_g, w_uq, w_ukv):
    B, S, _ = cq.shape
    pos = jnp.arange(S, dtype=jnp.int32)
    q = (rmsnorm(cq, q_norm_g) @ w_uq).reshape(B, S, MLA_HEADS, MLA_NOPE_DIM + MLA_ROPE_DIM)
    q = jnp.concatenate([q[..., :MLA_NOPE_DIM], rope(q[..., MLA_NOPE_DIM:], pos)], axis=-1)
    kv = (rmsnorm(ckv, kv_norm_g) @ w_ukv).reshape(B, S, MLA_HEADS, MLA_NOPE_DIM + MLA_V_DIM)
    k_r = rope(kr[:, :, None, :], pos)
    k = jnp.concatenate([kv[..., :MLA_NOPE_DIM],
                         jnp.broadcast_to(k_r, (B, S, MLA_HEADS, MLA_ROPE_DIM))], axis=-1)
    v = kv[..., MLA_NOPE_DIM:]
    return attend_blocks(q, k, v)


def mlstm_chunkwise(q, k, v, log_i, log_f):
    B, S, H, DK = q.shape
    DV = v.shape[-1]
    L = MLSTM_CHUNK
    nc = S // L
    f32 = jnp.float32

    def chunks(a):
        return a.astype(f32).reshape(B, nc, L, H, a.shape[-1]).transpose(1, 0, 3, 2, 4)

    def gchunks(a):
        return a.astype(f32).reshape(B, nc, L, H).transpose(1, 0, 3, 2)

    qc = chunks(q) * (DK ** -0.5)
    kc, vc = chunks(k), chunks(v)
    lic, lfc = gchunks(log_i), gchunks(log_f)
    tril = jnp.tril(jnp.ones((L, L), dtype=bool))

    def step(carry, inp):
        C, n, m = carry
        qi, ki, vi, li, lf = inp
        b = jnp.cumsum(lf, axis=-1)
        dmat = jnp.where(tril, b[..., :, None] - b[..., None, :] + li[..., None, :], -jnp.inf)
        inter = b + m[..., None]
        m_t = jnp.maximum(inter, jnp.max(dmat, axis=-1))
        w_intra = jnp.exp(dmat - m_t[..., None])
        w_state = jnp.exp(inter - m_t)
        s = jnp.einsum('bhtd,bhsd->bhts', qi, ki) * w_intra
        num = jnp.einsum('bhts,bhse->bhte', s, vi) + w_state[..., None] * jnp.einsum('bhtd,bhde->bhte', qi, C)
        den = jnp.sum(s, axis=-1) + w_state * jnp.einsum('bhtd,bhd->bht', qi, n)
        h = num / jnp.maximum(jnp.abs(den), jnp.exp(-m_t))[..., None]
        b_last = b[..., -1]
        a = b_last[..., None] - b + li
        m_new = jnp.maximum(b_last + m, jnp.max(a, axis=-1))
        w_s = jnp.exp(a - m_new[..., None])
        decay = jnp.exp(b_last + m - m_new)
        C = decay[..., None, None] * C + jnp.einsum('bhsd,bhse->bhde', ki * w_s[..., None], vi)
        n = decay[..., None] * n + jnp.einsum('bhs,bhsd->bhd', w_s, ki)
        return (C, n, m_new), h

    init = (jnp.zeros((B, H, DK, DV), f32), jnp.zeros((B, H, DK), f32), jnp.zeros((B, H), f32))
    _, hs = lax.scan(step, init, (qc, kc, vc, lic, lfc))
    return hs.transpose(1, 0, 3, 2, 4).reshape(B, S, H, DV)


def mlstm_branch(qk_raw, v_raw, o_raw, gates, conv_m_w, conv_m_b, mh_norm_g):
    B, S, _ = qk_raw.shape
    qk = jax.nn.silu(dwconv(qk_raw, conv_m_w, conv_m_b))
    q = qk[..., :D_MLSTM].reshape(B, S, MLSTM_HEADS, MLSTM_HEAD_DIM)
    k = qk[..., D_MLSTM:].reshape(B, S, MLSTM_HEADS, MLSTM_HEAD_DIM)
    v = v_raw.reshape(B, S, MLSTM_HEADS, MLSTM_HEAD_DIM)
    g = gates.astype(jnp.float32)
    i_f, f_f, i_b, f_b = jnp.split(g, 4, axis=-1)
    h_fwd = mlstm_chunkwise(q, k, v, i_f, jax.nn.log_sigmoid(f_f))
    flip = lambda a: jnp.flip(a, axis=1)
    h_bwd = flip(mlstm_chunkwise(flip(q), flip(k), flip(v), flip(i_b), flip(jax.nn.log_sigmoid(f_b))))
    h = h_fwd + h_bwd
    mu = jnp.mean(h, axis=-1, keepdims=True)
    var = jnp.mean(jnp.square(h - mu), axis=-1, keepdims=True)
    hn = (h - mu) * lax.rsqrt(var + EPS) * mh_norm_g.astype(jnp.float32).reshape(MLSTM_HEADS, MLSTM_HEAD_DIM)
    return hn.reshape(B, S, D_MLSTM).astype(o_raw.dtype) * jax.nn.sigmoid(o_raw)


def conv_ffn(h, w_up, cw, cb, w_down):
    u = dwconv(h @ w_up, cw, cb)
    a, g = jnp.split(u, 2, axis=-1)
    return (jax.nn.silu(g) * a) @ w_down


def encoder_trunk(x, c, norm1_g, w_ada, b_ada, w_in, b_gate, q_norm_g, kv_norm_g, w_uq, w_ukv,
                  conv_m_w, conv_m_b, mh_norm_g, w_out, norm2_g, w_up, conv_f_w, conv_f_b, w_down, final_g):
    for l in range(DEPTH):
        mod = jax.nn.silu(c) @ w_ada[l] + b_ada[l]
        sh1, sc1, g1, sh2, sc2, g2 = jnp.split(mod, 6, axis=-1)
        h = modulate(rmsnorm(x, norm1_g[l]), sh1, sc1)
        proj = h @ w_in[l]
        cq, ckv, kr, qk_raw, v_m, o_m, gates = jnp.split(proj, IN_SPLITS, axis=-1)
        attn = mla_branch(cq, ckv, kr, q_norm_g[l], kv_norm_g[l], w_uq[l], w_ukv[l])
        mem = mlstm_branch(qk_raw, v_m, o_m, gates + b_gate[l], conv_m_w[l], conv_m_b[l], mh_norm_g[l])
        x = x + g1[:, None, :] * (jnp.concatenate([attn, mem], axis=-1) @ w_out[l])
        h = modulate(rmsnorm(x, norm2_g[l]), sh2, sc2)
        x = x + g2[:, None, :] * conv_ffn(h, w_up[l], conv_f_w[l], conv_f_b[l], w_down[l])
    return rmsnorm(x, final_g)


def setup_inputs(seed: int = 0) -> dict:
    key = jax.random.key(seed)
    ks = jax.random.split(key, 28)
    f32 = jnp.float32

    def nrm(k, shape, scale):
        return jax.random.normal(k, shape, f32) * scale

    def gain(k, shape):
        return 1.0 + 0.02 * jax.random.normal(k, shape, f32)

    fb_fwd = jnp.linspace(3.0, 6.0, MLSTM_HEADS, dtype=f32)[None, :] + nrm(ks[22], (DEPTH, MLSTM_HEADS), 0.1)
    fb_bwd = jnp.linspace(3.0, 6.0, MLSTM_HEADS, dtype=f32)[None, :] + nrm(ks[23], (DEPTH, MLSTM_HEADS), 0.1)
    b_gate = jnp.concatenate([nrm(ks[24], (DEPTH, MLSTM_HEADS), 0.1), fb_fwd,
                              nrm(ks[25], (DEPTH, MLSTM_HEADS), 0.1), fb_bwd], axis=-1)
    return {
        'x_prompt': nrm(ks[0], (BATCH, SEQ, D_MODEL), 1.0),
        'x_sample': nrm(ks[1], (DEC_BATCH, DEC_SEQ, D_MODEL), 1.0),
        'c_prompt': nrm(ks[2], (BATCH, D_MODEL), 1.0),
        'c_sample': nrm(ks[3], (DEC_BATCH, D_MODEL), 1.0),
        'norm1_g': gain(ks[4], (DEPTH, D_MODEL)),
        'w_ada': nrm(ks[5], (DEPTH, D_MODEL, 6 * D_MODEL), 0.5 * D_MODEL ** -0.5),
        'b_ada': nrm(ks[6], (DEPTH, 6 * D_MODEL), 0.02),
        'w_in': nrm(ks[7], (DEPTH, D_MODEL, N_IN), D_MODEL ** -0.5),
        'b_gate': b_gate,
        'q_norm_g': gain(ks[8], (DEPTH, Q_LORA_RANK)),
        'kv_norm_g': gain(ks[9], (DEPTH, KV_LORA_RANK)),
        'w_uq': nrm(ks[10], (DEPTH, Q_LORA_RANK, MLA_HEADS * (MLA_NOPE_DIM + MLA_ROPE_DIM)), Q_LORA_RANK ** -0.5),
        'w_ukv': nrm(ks[11], (DEPTH, KV_LORA_RANK, MLA_HEADS * (MLA_NOPE_DIM + MLA_V_DIM)), KV_LORA_RANK ** -0.5),
        'conv_m_w': nrm(ks[12], (DEPTH, CONV_WIDTH, 2 * D_MLSTM), CONV_WIDTH ** -0.5),
        'conv_m_b': nrm(ks[13], (DEPTH, 2 * D_MLSTM), 0.02),
        'mh_norm_g': gain(ks[14], (DEPTH, D_MLSTM)),
        'w_out': nrm(ks[15], (DEPTH, D_MIX, D_MODEL), D_MIX ** -0.5),
        'norm2_g': gain(ks[16], (DEPTH, D_MODEL)),
        'w_up': nrm(ks[17], (DEPTH, D_MODEL, 2 * D_FF), D_MODEL ** -0.5),
        'conv_f_w': nrm(ks[18], (DEPTH, CONV_WIDTH, 2 * D_FF), CONV_WIDTH ** -0.5),
        'conv_f_b': nrm(ks[19], (DEPTH, 2 * D_FF), 0.02),
        'w_down': nrm(ks[20], (DEPTH, D_FF, D_MODEL), D_FF ** -0.5),
        'final_g': gain(ks[21], (D_MODEL,)),
    }


def reference(x_prompt, x_sample, c_prompt, c_sample, norm1_g, w_ada, b_ada, w_in, b_gate, q_norm_g,
              kv_norm_g, w_uq, w_ukv, conv_m_w, conv_m_b, mh_norm_g, w_out, norm2_g, w_up, conv_f_w,
              conv_f_b, w_down, final_g):
    y_prompt = encoder_trunk(x_prompt, c_prompt, norm1_g, w_ada, b_ada, w_in, b_gate, q_norm_g, kv_norm_g,
                             w_uq, w_ukv, conv_m_w, conv_m_b, mh_norm_g, w_out, norm2_g, w_up, conv_f_w,
                             conv_f_b, w_down, final_g)
    y_sample = encoder_trunk(x_sample, c_sample, norm1_g, w_ada, b_ada, w_in, b_gate, q_norm_g, kv_norm_g,
                             w_uq, w_ukv, conv_m_w, conv_m_b, mh_norm_g, w_out, norm2_g, w_up, conv_f_w,
                             conv_f_b, w_down, final_g)
    return (y_prompt, y_sample)
```

```python
import functools
import math

import jax
import jax.numpy as jnp
from jax import lax
from jax.experimental import pallas as pl
from jax.experimental.pallas import tpu as pltpu

F32 = jnp.float32
BF16 = jnp.bfloat16

D_MODEL = 1024
N_HEADS = 8
D_NOPE = 64
D_ROPE = 32
D_V = 64
Q_RANK = 384
KV_RANK = 256
M_HEADS = 4
M_DIM = 128
D_MLSTM = M_HEADS * M_DIM
CHUNK = 128
D_FF = 2816
ROPE_THETA = 10000.0
EPS = 1e-6
LANES = 128
SUBLANES = 8
FF_CHUNK = 256
VMEM_LIMIT = 56 * 1024 * 1024

_A_W = Q_RANK + KV_RANK + 2 * LANES
_QK0 = _A_W
_V0 = _QK0 + 2 * D_MLSTM
_O0 = _V0 + D_MLSTM
_G0 = _O0 + D_MLSTM
_N_PREP = _G0 + LANES


def _dot(a, b):
    return jnp.dot(a, b, preferred_element_type=F32)


def _dot_nt(a, b):
    return lax.dot_general(a, b, (((1,), (1,)), ((), ())), preferred_element_type=F32)


def _dot_tn(a, b):
    return lax.dot_general(a, b, (((0,), (0,)), ((), ())), preferred_element_type=F32)


def _rmsnorm(x, g):
    return x * lax.rsqrt(jnp.mean(x * x, axis=-1, keepdims=True) + EPS) * g


def _params(sem):
    return pltpu.CompilerParams(dimension_semantics=sem, vmem_limit_bytes=VMEM_LIMIT)


def _const_spec(shape):
    nd = len(shape)
    return pl.BlockSpec(shape, lambda *_: (0,) * nd, pipeline_mode=pl.Buffered(1))


def _ada_kernel(c_ref, w_ref, b_ref, o_ref):
    c = c_ref[...]
    a = (c * jax.nn.sigmoid(c)).astype(BF16)
    o_ref[...] = _dot(a, w_ref[...].astype(BF16)) + b_ref[...]


def _ada_mod(c, w_ada, b_ada):
    B = c.shape[0]
    cp = jnp.zeros((SUBLANES, D_MODEL), F32).at[:B].set(c)
    n = 6 * D_MODEL
    out = pl.pallas_call(
        _ada_kernel,
        out_shape=jax.ShapeDtypeStruct((SUBLANES, n), F32),
        grid=(6,),
        in_specs=[
            pl.BlockSpec((SUBLANES, D_MODEL), lambda j: (0, 0)),
            pl.BlockSpec((D_MODEL, D_MODEL), lambda j: (0, j)),
            pl.BlockSpec((1, D_MODEL), lambda j: (0, j)),
        ],
        out_specs=pl.BlockSpec((SUBLANES, D_MODEL), lambda j: (0, j)),
        compiler_params=_params(("parallel",)),
        name="ada_mod",
    )(cp, w_ada, b_ada.reshape(1, n))
    return out[:B].reshape(B, 1, n)


def _in_proj_kernel(x_ref, mod_ref, n1_ref, w_ref, bg_ref, qn_ref, kvn_ref, wq_ref, wkv_ref,
                    cos_ref, sin_ref, one_ref,
                    q_ref, k_ref, v_ref, qk_ref, vm_ref, om_ref, g_ref):
    x = x_ref[0]
    sh1 = mod_ref[0, :, 0:D_MODEL]
    sc1 = mod_ref[0, :, D_MODEL:2 * D_MODEL]
    h = (_rmsnorm(x, n1_ref[...]) * (1.0 + sc1) + sh1).astype(BF16)

    pa = _dot(h, w_ref[:, 0:_A_W])
    qk_ref[0] = _dot(h, w_ref[:, _QK0:_V0])
    vm_ref[0] = _dot(h, w_ref[:, _V0:_O0]).astype(BF16)
    om_ref[0] = _dot(h, w_ref[:, _O0:_G0])
    g_ref[0] = _dot(h, w_ref[:, _G0:_N_PREP]) + bg_ref[...]

    cos = cos_ref[...]
    sin = sin_ref[...]
    cos8 = jnp.concatenate([cos] * N_HEADS, axis=1)
    sin8 = jnp.concatenate([sin] * N_HEADS, axis=1)
    hw = N_HEADS * LANES

    cqn = _rmsnorm(pa[:, 0:Q_RANK], qn_ref[...]).astype(BF16)
    qq = _dot(cqn, wq_ref[...])
    q = (qq[:, 0:hw] * cos8 + qq[:, hw:2 * hw] * sin8).astype(BF16)

    ckvn = _rmsnorm(pa[:, Q_RANK:Q_RANK + KV_RANK], kvn_ref[...]).astype(BF16)
    kk = _dot(ckvn, wkv_ref[...])
    kr0 = Q_RANK + KV_RANK
    kr = pa[:, kr0:kr0 + LANES] * cos + pa[:, kr0 + LANES:kr0 + 2 * LANES] * sin
    k = (kk[:, 0:hw] + jnp.concatenate([kr] * N_HEADS, axis=1)).astype(BF16)
    v = (kk[:, hw:2 * hw] + one_ref[...]).astype(BF16)
    for hd in range(N_HEADS):
        sl = slice(hd * LANES, (hd + 1) * LANES)
        q_ref[0, hd] = q[:, sl]
        k_ref[0, hd] = k[:, sl]
        v_ref[0, hd] = v[:, sl]


def _in_proj(x, mod, prep, tm):
    B, S, _ = x.shape
    grid = (B, S // tm)
    row = lambda b, i: (b, i, 0)
    hrow = lambda b, i: (b, 0, i, 0)
    head_shape = jax.ShapeDtypeStruct((B, N_HEADS, S, LANES), BF16)
    return pl.pallas_call(
        _in_proj_kernel,
        out_shape=(
            head_shape, head_shape, head_shape,
            jax.ShapeDtypeStruct((B, S, 2 * D_MLSTM), F32),
            jax.ShapeDtypeStruct((B, S, D_MLSTM), BF16),
            jax.ShapeDtypeStruct((B, S, D_MLSTM), F32),
            jax.ShapeDtypeStruct((B, S, LANES), F32),
        ),
        grid=grid,
        in_specs=[
            pl.BlockSpec((1, tm, D_MODEL), row),
            pl.BlockSpec((1, 1, 6 * D_MODEL), lambda b, i: (b, 0, 0)),
            _const_spec((1, D_MODEL)),
            _const_spec((D_MODEL, _N_PREP)),
            _const_spec((1, LANES)),
            _const_spec((1, Q_RANK)),
            _const_spec((1, KV_RANK)),
            _const_spec((Q_RANK, 2 * N_HEADS * LANES)),
            _const_spec((KV_RANK, 2 * N_HEADS * LANES)),
            pl.BlockSpec((tm, LANES), lambda b, i: (i, 0)),
            pl.BlockSpec((tm, LANES), lambda b, i: (i, 0)),
            _const_spec((1, N_HEADS * LANES)),
        ],
        out_specs=(
            pl.BlockSpec((1, N_HEADS, tm, LANES), hrow),
            pl.BlockSpec((1, N_HEADS, tm, LANES), hrow),
            pl.BlockSpec((1, N_HEADS, tm, LANES), hrow),
            pl.BlockSpec((1, tm, 2 * D_MLSTM), row),
            pl.BlockSpec((1, tm, D_MLSTM), row),
            pl.BlockSpec((1, tm, D_MLSTM), row),
            pl.BlockSpec((1, tm, LANES), row),
        ),
        compiler_params=_params(("parallel", "parallel")),
        name="in_proj",
    )(x, mod, prep["norm1_g"], prep["w_in"], prep["b_gate"], prep["q_norm_g"], prep["kv_norm_g"],
      prep["w_q"], prep["w_kv"], prep["cos"], prep["sin"], prep["v_one"])


def _attn_kernel(q_ref, k_ref, v_ref, o_ref, *, tk):
    q = q_ref[0, 0]
    tq = q.shape[0]
    S = k_ref.shape[2]

    def body(j, carry):
        m, acc = carry
        off = pl.multiple_of(j * tk, tk)
        kc = k_ref[0, 0, pl.ds(off, tk), :]
        vc = v_ref[0, 0, pl.ds(off, tk), :]
        s = _dot_nt(q, kc)
        m_new = jnp.maximum(m, jnp.max(s, axis=1, keepdims=True))
        alpha = jnp.exp2(m - m_new)
        p = jnp.exp2(s - m_new).astype(BF16)
        return m_new, alpha * acc + _dot(p, vc)

    m0 = jnp.full((tq, 1), -jnp.inf, F32)
    acc0 = jnp.zeros((tq, LANES), F32)
    _, acc = lax.fori_loop(0, S // tk, body, (m0, acc0))
    o_ref[0, 0] = (acc / acc[:, D_V:D_V + 1]).astype(BF16)


def _attention(q, k, v, tq, tk):
    B, H, S, _ = q.shape
    return pl.pallas_call(
        functools.partial(_attn_kernel, tk=tk),
        out_shape=jax.ShapeDtypeStruct((B, H, S, LANES), BF16),
        grid=(B, H, S // tq),
        in_specs=[
            pl.BlockSpec((1, 1, tq, LANES), lambda b, h, i: (b, h, i, 0)),
            pl.BlockSpec((1, 1, S, LANES), lambda b, h, i: (b, h, 0, 0), pipeline_mode=pl.Buffered(1)),
            pl.BlockSpec((1, 1, S, LANES), lambda b, h, i: (b, h, 0, 0), pipeline_mode=pl.Buffered(1)),
        ],
        out_specs=pl.BlockSpec((1, 1, tq, LANES), lambda b, h, i: (b, h, i, 0)),
        compiler_params=_params(("parallel", "parallel", "parallel")),
        name="attention",
    )(q, k, v)


def _split3(x):
    hi = x.astype(BF16)
    r1 = x - hi.astype(F32)
    mid = r1.astype(BF16)
    lo = (r1 - mid.astype(F32)).astype(BF16)
    return hi, mid, lo


def _log_sigmoid(x):
    return jnp.minimum(x, 0.0) - jnp.log1p(jnp.exp(-jnp.abs(x)))


def _mlstm_direction(qk_ref, prev_ref, next_ref, v_ref, g_ref, cw_ref, cb_ref, h_ref,
                     c_scr, n_scr, m_scr, *, chunk_idx, n_chunks, reverse):
    L = CHUNK
    x = qk_ref[0]
    row = lax.broadcasted_iota(jnp.int32, (L, 1), 0)
    prev_row = jnp.where(chunk_idx > 0, prev_ref[0, SUBLANES - 1:SUBLANES, :], 0.0)
    next_row = jnp.where(chunk_idx < n_chunks - 1, next_ref[0, 0:1, :], 0.0)
    xm = jnp.where(row == 0, prev_row, pltpu.roll(x, 1, axis=0))
    xp = jnp.where(row == L - 1, next_row, pltpu.roll(x, L - 1, axis=0))
    conv = cw_ref[0:1, :] * xm + cw_ref[1:2, :] * x + cw_ref[2:3, :] * xp + cb_ref[...]
    qk = conv * jax.nn.sigmoid(conv)

    g = g_ref[0]
    gt = g.T
    ti = lax.broadcasted_iota(jnp.int32, (L, L), 0)
    si = lax.broadcasted_iota(jnp.int32, (L, L), 1)
    vis = (si >= ti) if reverse else (si <= ti)
    vis_bf = jnp.where(vis, 1.0, 0.0).astype(BF16)
    vis_t_bf = jnp.where((ti >= si) if reverse else (ti <= si), 1.0, 0.0).astype(BF16)
    ls = _log_sigmoid(g)
    ls_t = _log_sigmoid(gt)
    b_cols = sum(_dot(vis_bf, piece) for piece in _split3(ls))
    b_rows = sum(_dot(piece, vis_t_bf) for piece in _split3(ls_t))

    i_off = 2 * M_HEADS if reverse else 0
    f_off = i_off + M_HEADS
    last = 0 if reverse else L - 1
    scale = M_DIM ** -0.5
    for hd in range(M_HEADS):
        sl = slice(hd * M_DIM, (hd + 1) * M_DIM)
        q = qk[:, sl] * scale
        k = qk[:, D_MLSTM + hd * M_DIM:D_MLSTM + (hd + 1) * M_DIM]
        v = v_ref[0, :, sl]
        qb = q.astype(BF16)
        kb = k.astype(BF16)
        b_c = b_cols[:, f_off + hd:f_off + hd + 1]
        b_r = b_rows[f_off + hd:f_off + hd + 1, :]
        li_c = g[:, i_off + hd:i_off + hd + 1]
        li_r = gt[i_off + hd:i_off + hd + 1, :]
        m_prev = m_scr[hd]
        c_prev = c_scr[hd]
        n_prev = n_scr[hd]

        dmat = jnp.where(vis, b_c - b_r + li_r, -jnp.inf)
        inter = b_c + m_prev
        m_t = jnp.maximum(inter, jnp.max(dmat, axis=1, keepdims=True))
        w_intra = jnp.exp(dmat - m_t)
        w_state = jnp.exp(inter - m_t)
        s = _dot_nt(qb, kb) * w_intra
        num = _dot(s.astype(BF16), v) + w_state * _dot(qb, c_prev.astype(BF16))
        den = jnp.sum(s, axis=1, keepdims=True) + w_state * jnp.sum(q * n_prev, axis=1, keepdims=True)
        h_ref[0, :, sl] = num / jnp.maximum(jnp.abs(den), jnp.exp(-m_t))

        b_last = b_c[last:last + 1, :]
        a = b_last - b_c + li_c
        m_new = jnp.maximum(b_last + m_prev, jnp.max(a, axis=0, keepdims=True))
        w_s = jnp.exp(a - m_new)
        decay = jnp.exp(b_last + m_prev - m_new)
        kw = k * w_s
        c_scr[hd] = decay * c_prev + _dot_tn(kw.astype(BF16), v)
        n_scr[hd] = decay * n_prev + jnp.sum(kw, axis=0, keepdims=True)
        m_scr[hd] = m_new


def _mlstm_kernel(qkf_ref, pf_ref, nf_ref, vf_ref, gf_ref,
                  qkb_ref, pb_ref, nb_ref, vb_ref, gb_ref, cw_ref, cb_ref,
                  hf_ref, hb_ref, cf_scr, nf_scr, mf_scr, cb_scr, nb_scr, mb_scr):
    i = pl.program_id(1)
    nc = pl.num_programs(1)

    @pl.when(i == 0)
    def _():
        for r in (cf_scr, nf_scr, mf_scr, cb_scr, nb_scr, mb_scr):
            r[...] = jnp.zeros_like(r)

    _mlstm_direction(qkf_ref, pf_ref, nf_ref, vf_ref, gf_ref, cw_ref, cb_ref, hf_ref,
                     cf_scr, nf_scr, mf_scr, chunk_idx=i, n_chunks=nc, reverse=False)
    _mlstm_direction(qkb_ref, pb_ref, nb_ref, vb_ref, gb_ref, cw_ref, cb_ref, hb_ref,
                     cb_scr, nb_scr, mb_scr, chunk_idx=nc - 1 - i, n_chunks=nc, reverse=True)


def _mlstm(qk_raw, v_m, gates, conv_w, conv_b):
    B, S, _ = qk_raw.shape
    nc = S // CHUNK
    per = CHUNK // SUBLANES
    nsub = S // SUBLANES

    def specs(cidx):
        main = lambda b, i: (b, cidx(i), 0)
        prev = lambda b, i: (b, jnp.maximum(cidx(i) * per - 1, 0), 0)
        nxt = lambda b, i: (b, jnp.minimum((cidx(i) + 1) * per, nsub - 1), 0)
        return [
            pl.BlockSpec((1, CHUNK, 2 * D_MLSTM), main),
            pl.BlockSpec((1, SUBLANES, 2 * D_MLSTM), prev),
            pl.BlockSpec((1, SUBLANES, 2 * D_MLSTM), nxt),
            pl.BlockSpec((1, CHUNK, D_MLSTM), main),
            pl.BlockSpec((1, CHUNK, LANES), main),
        ]

    fwd = lambda i: i
    bwd = lambda i: nc - 1 - i
    state = [pltpu.VMEM((M_HEADS, M_DIM, M_DIM), F32), pltpu.VMEM((M_HEADS, 1, M_DIM), F32),
             pltpu.VMEM((M_HEADS, 1, 1), F32)]
    h_shape = jax.ShapeDtypeStruct((B, S, D_MLSTM), F32)
    return pl.pallas_call(
        _mlstm_kernel,
        out_shape=(h_shape, h_shape),
        grid=(B, nc),
        in_specs=specs(fwd) + specs(bwd) + [_const_spec((3, 2 * D_MLSTM)), _const_spec((1, 2 * D_MLSTM))],
        out_specs=(
            pl.BlockSpec((1, CHUNK, D_MLSTM), lambda b, i: (b, i, 0)),
            pl.BlockSpec((1, CHUNK, D_MLSTM), lambda b, i: (b, nc - 1 - i, 0)),
        ),
        scratch_shapes=state + state,
        compiler_params=_params(("parallel", "arbitrary")),
        name="mlstm",
    )(qk_raw, qk_raw, qk_raw, v_m, gates, qk_raw, qk_raw, qk_raw, v_m, gates, conv_w, conv_b)


def _out_proj_kernel(x_ref, mod_ref, a_ref, hf_ref, hb_ref, om_ref, mg_ref, wa_ref, wm_ref, o_ref):
    g1 = mod_ref[0, :, 2 * D_MODEL:3 * D_MODEL]
    attn = jnp.concatenate([a_ref[0, hd] for hd in range(N_HEADS)], axis=1)
    y = _dot(attn, wa_ref[...])
    h = hf_ref[0] + hb_ref[0]
    om = om_ref[0]
    parts = []
    for hd in range(M_HEADS):
        sl = slice(hd * M_DIM, (hd + 1) * M_DIM)
        hh = h[:, sl]
        mu = jnp.mean(hh, axis=1, keepdims=True)
        d = hh - mu
        var = jnp.mean(d * d, axis=1, keepdims=True)
        parts.append(d * lax.rsqrt(var + EPS) * mg_ref[:, sl] * jax.nn.sigmoid(om[:, sl]))
    mem = jnp.concatenate(parts, axis=1).astype(BF16)
    y = y + _dot(mem, wm_ref[...])
    o_ref[0] = x_ref[0] + g1 * y


def _out_proj(x, mod, attn, h_f, h_b, o_m, prep, tm):
    B, S, _ = x.shape
    row = lambda b, i: (b, i, 0)
    return pl.pallas_call(
        _out_proj_kernel,
        out_shape=jax.ShapeDtypeStruct((B, S, D_MODEL), F32),
        grid=(B, S // tm),
        in_specs=[
            pl.BlockSpec((1, tm, D_MODEL), row),
            pl.BlockSpec((1, 1, 6 * D_MODEL), lambda b, i: (b, 0, 0)),
            pl.BlockSpec((1, N_HEADS, tm, LANES), lambda b, i: (b, 0, i, 0)),
            pl.BlockSpec((1, tm, D_MLSTM), row),
            pl.BlockSpec((1, tm, D_MLSTM), row),
            pl.BlockSpec((1, tm, D_MLSTM), row),
            _const_spec((1, D_MLSTM)),
            _const_spec((N_HEADS * LANES, D_MODEL)),
            _const_spec((D_MLSTM, D_MODEL)),
        ],
        out_specs=pl.BlockSpec((1, tm, D_MODEL), row),
        compiler_params=_params(("parallel", "parallel")),
        name="out_proj",
    )(x, mod, attn, h_f, h_b, o_m, prep["mh_norm_g"], prep["w_out_attn"], prep["w_out_mem"])


def _ffn_kernel(x_ref, xp_ref, xn_ref, mod_ref, n2_ref, wu_ref, cw_ref, cb_ref, wd_ref, fg_ref, o_ref):
    i = pl.program_id(1)
    nt = pl.num_programs(1)
    tm = x_ref.shape[1]
    rows = tm + 2 * SUBLANES
    sh2 = mod_ref[0, :, 3 * D_MODEL:4 * D_MODEL]
    sc2 = mod_ref[0, :, 4 * D_MODEL:5 * D_MODEL]
    g2 = mod_ref[0, :, 5 * D_MODEL:6 * D_MODEL]

    def norm_mod(r):
        return _rmsnorm(r, n2_ref[...]) * (1.0 + sc2) + sh2

    x = x_ref[0]
    hp = jnp.where(i > 0, norm_mod(xp_ref[0]), 0.0)
    hn = jnp.where(i < nt - 1, norm_mod(xn_ref[0]), 0.0)
    hext = jnp.concatenate([hp, norm_mod(x), hn], axis=0).astype(BF16)

    def conv(u, c0):
        cs = slice(c0, c0 + FF_CHUNK)
        um = pltpu.roll(u, 1, axis=0)
        up = pltpu.roll(u, rows - 1, axis=0)
        full = cw_ref[0:1, cs] * um + cw_ref[1:2, cs] * u + cw_ref[2:3, cs] * up + cb_ref[:, cs]
        return full[SUBLANES:SUBLANES + tm]

    acc = jnp.zeros((tm, D_MODEL), F32)
    for j in range(D_FF // FF_CHUNK):
        c0 = j * FF_CHUNK
        a = conv(_dot(hext, wu_ref[:, c0:c0 + FF_CHUNK]), c0)
        g = conv(_dot(hext, wu_ref[:, D_FF + c0:D_FF + c0 + FF_CHUNK]), D_FF + c0)
        act = (g * jax.nn.sigmoid(g) * a).astype(BF16)
        acc = acc + _dot(act, wd_ref[c0:c0 + FF_CHUNK, :])
    o_ref[0] = _rmsnorm(x + g2 * acc, fg_ref[...])


def _conv_ffn(x, mod, prep, tm):
    B, S, _ = x.shape
    per = tm // SUBLANES
    nsub = S // SUBLANES
    row = lambda b, i: (b, i, 0)
    return pl.pallas_call(
        _ffn_kernel,
        out_shape=jax.ShapeDtypeStruct((B, S, D_MODEL), F32),
        grid=(B, S // tm),
        in_specs=[
            pl.BlockSpec((1, tm, D_MODEL), row),
            pl.BlockSpec((1, SUBLANES, D_MODEL), lambda b, i: (b, jnp.maximum(i * per - 1, 0), 0)),
            pl.BlockSpec((1, SUBLANES, D_MODEL), lambda b, i: (b, jnp.minimum((i + 1) * per, nsub - 1), 0)),
            pl.BlockSpec((1, 1, 6 * D_MODEL), lambda b, i: (b, 0, 0)),
            _const_spec((1, D_MODEL)),
            _const_spec((D_MODEL, 2 * D_FF)),
            _const_spec((3, 2 * D_FF)),
            _const_spec((1, 2 * D_FF)),
            _const_spec((D_FF, D_MODEL)),
            _const_spec((1, D_MODEL)),
        ],
        out_specs=pl.BlockSpec((1, tm, D_MODEL), row),
        compiler_params=_params(("parallel", "parallel")),
        name="conv_ffn",
    )(x, x, x, mod, prep["norm2_g"], prep["w_up"], prep["conv_f_w"], prep["conv_f_b"],
      prep["w_down"], prep["final_g"])


def _head_groups(w, n_heads, width, pieces):
    rows = w.shape[0]
    wh = w.reshape(rows, n_heads, width)
    out = jnp.zeros((rows, n_heads, LANES), F32)
    for src, n, dst, sign in pieces:
        out = out.at[:, :, dst:dst + n].set(sign * wh[:, :, src:src + n])
    return out.reshape(rows, n_heads * LANES)


def _prepare(S, norm1_g, w_in, b_gate, q_norm_g, kv_norm_g, w_uq, w_ukv, conv_m_w, conv_m_b,
             mh_norm_g, w_out, norm2_g, w_up, conv_f_w, conv_f_b, w_down, final_g):
    half = D_ROPE // 2
    r1 = D_NOPE
    r2 = D_NOPE + half
    c0, c1, c2 = Q_RANK, Q_RANK + KV_RANK, Q_RANK + KV_RANK + D_ROPE
    c3 = c2 + 2 * D_MLSTM
    c4 = c3 + D_MLSTM
    c5 = c4 + D_MLSTM

    kr = w_in[:, c1:c2]
    zeros = lambda n: jnp.zeros((D_MODEL, n), F32)
    kr_main = jnp.concatenate([zeros(r1), kr[:, :half], kr[:, half:], zeros(LANES - r2 - half)], axis=1)
    kr_swap = jnp.concatenate([zeros(r1), -kr[:, half:], kr[:, :half], zeros(LANES - r2 - half)], axis=1)
    gates = jnp.concatenate([w_in[:, c5:], zeros(LANES - 4 * M_HEADS)], axis=1)
    w_prep = jnp.concatenate([w_in[:, :c1], kr_main, kr_swap, w_in[:, c2:c5], gates], axis=1)

    qs = (D_NOPE + D_ROPE) ** -0.5 * math.log2(math.e)
    dq = D_NOPE + D_ROPE
    wq_main = _head_groups(w_uq, N_HEADS, dq, [(0, D_NOPE, 0, 1.0), (D_NOPE, half, r1, 1.0),
                                                (D_NOPE + half, half, r2, 1.0)])
    wq_swap = _head_groups(w_uq, N_HEADS, dq, [(D_NOPE + half, half, r1, -1.0), (D_NOPE, half, r2, 1.0)])
    w_q = jnp.concatenate([wq_main, wq_swap], axis=1) * qs

    dkv = D_NOPE + D_V
    wk = _head_groups(w_ukv, N_HEADS, dkv, [(0, D_NOPE, 0, 1.0)])
    wv = _head_groups(w_ukv, N_HEADS, dkv, [(D_NOPE, D_V, 0, 1.0)])
    w_kv = jnp.concatenate([wk, wv], axis=1)

    inv = 1.0 / (ROPE_THETA ** (jnp.arange(half, dtype=F32) * (2.0 / D_ROPE)))
    ang = jnp.arange(S, dtype=jnp.int32).astype(F32)[:, None] * inv[None, :]
    ones = jnp.ones((S, r1), F32)
    pad = jnp.zeros((S, LANES - r2 - half), F32)
    cos = jnp.concatenate([ones, jnp.cos(ang), jnp.cos(ang), pad], axis=1)
    sin = jnp.concatenate([0.0 * ones, jnp.sin(ang), jnp.sin(ang), pad], axis=1)

    v_one = jnp.zeros((N_HEADS, LANES), F32).at[:, D_V].set(1.0).reshape(1, N_HEADS * LANES)
    b_g = jnp.concatenate([b_gate, jnp.zeros((LANES - 4 * M_HEADS,), F32)]).reshape(1, LANES)

    d_attn = N_HEADS * D_V
    w_oa = jnp.zeros((N_HEADS, LANES, D_MODEL), F32).at[:, :D_V, :].set(
        w_out[:d_attn].reshape(N_HEADS, D_V, D_MODEL)).reshape(N_HEADS * LANES, D_MODEL)

    return dict(
        norm1_g=norm1_g.reshape(1, -1), w_in=w_prep.astype(BF16), b_gate=b_g,
        q_norm_g=q_norm_g.reshape(1, -1), kv_norm_g=kv_norm_g.reshape(1, -1),
        w_q=w_q.astype(BF16), w_kv=w_kv.astype(BF16), cos=cos, sin=sin, v_one=v_one,
        conv_m_w=conv_m_w, conv_m_b=conv_m_b.reshape(1, -1), mh_norm_g=mh_norm_g.reshape(1, -1),
        w_out_attn=w_oa.astype(BF16), w_out_mem=w_out[d_attn:].astype(BF16),
        norm2_g=norm2_g.reshape(1, -1), w_up=w_up.astype(BF16), conv_f_w=conv_f_w,
        conv_f_b=conv_f_b.reshape(1, -1), w_down=w_down.astype(BF16), final_g=final_g.reshape(1, -1),
    )


def _tiles(S):
    tm = min(512, S)
    return tm, min(256, S), min(512, S)


def _trunk(x, c, w_ada, b_ada, prep):
    S = x.shape[1]
    tm, tq, tk = _tiles(S)
    mod = _ada_mod(c, w_ada, b_ada)
    q, k, v, qk_raw, v_m, o_m, gates = _in_proj(x, mod, prep, tm)
    attn = _attention(q, k, v, tq, tk)
    h_f, h_b = _mlstm(qk_raw, v_m, gates, prep["conv_m_w"], prep["conv_m_b"])
    x1 = _out_proj(x, mod, attn, h_f, h_b, o_m, prep, tm)
    return _conv_ffn(x1, mod, prep, tm)


def kernel(x_prompt, x_sample, c_prompt, c_sample, norm1_g, w_ada, b_ada, w_in, b_gate, q_norm_g, kv_norm_g, w_uq, w_ukv, conv_m_w, conv_m_b, mh_norm_g, w_out, norm2_g, w_up, conv_f_w, conv_f_b, w_down, final_g):
    assert w_ada.shape[0] == 1, "single-layer trunk"
    assert x_prompt.shape[1] == x_sample.shape[1]
    prep = _prepare(x_prompt.shape[1], norm1_g[0], w_in[0], b_gate[0], q_norm_g[0], kv_norm_g[0], w_uq[0],
                    w_ukv[0], conv_m_w[0], conv_m_b[0], mh_norm_g[0], w_out[0], norm2_g[0], w_up[0],
                    conv_f_w[0], conv_f_b[0], w_down[0], final_g)
    y_prompt = _trunk(x_prompt, c_prompt, w_ada[0], b_ada[0], prep)
    y_sample = _trunk(x_sample, c_sample, w_ada[0], b_ada[0], prep)
    return (y_prompt, y_sample)
```

```python
import functools
import math

import jax
import jax.numpy as jnp
from jax import lax
from jax.experimental import pallas as pl
from jax.experimental.pallas import tpu as pltpu

F32 = jnp.float32
BF16 = jnp.bfloat16

D_MODEL = 1024
N_HEADS = 8
D_NOPE = 64
D_ROPE = 32
D_V = 64
Q_RANK = 384
KV_RANK = 256
M_HEADS = 4
M_DIM = 128
D_MLSTM = M_HEADS * M_DIM
CHUNK = 128
D_FF = 2816
ROPE_THETA = 10000.0
EPS = 1e-6
LANES = 128
SUBLANES = 8
FF_CHUNK = 256
VMEM_LIMIT = 56 * 1024 * 1024

_A_W = Q_RANK + KV_RANK + 2 * LANES
_QK0 = _A_W
_V0 = _QK0 + 2 * D_MLSTM
_O0 = _V0 + D_MLSTM
_G0 = _O0 + D_MLSTM
_N_PREP = _G0 + LANES


def _dot(a, b):
    return jnp.dot(a, b, preferred_element_type=F32)


def _dot_nt(a, b):
    return lax.dot_general(a, b, (((1,), (1,)), ((), ())), preferred_element_type=F32)


def _dot_tn(a, b):
    return lax.dot_general(a, b, (((0,), (0,)), ((), ())), preferred_element_type=F32)


def _rmsnorm(x, g):
    return x * lax.rsqrt(jnp.mean(x * x, axis=-1, keepdims=True) + EPS) * g


def _params(sem):
    return pltpu.CompilerParams(dimension_semantics=sem, vmem_limit_bytes=VMEM_LIMIT)


def _const_spec(shape):
    nd = len(shape)
    return pl.BlockSpec(shape, lambda *_: (0,) * nd, pipeline_mode=pl.Buffered(1))


def _ada_kernel(c_ref, w_ref, b_ref, o_ref):
    c = c_ref[...]
    a = (c * jax.nn.sigmoid(c)).astype(BF16)
    o_ref[...] = _dot(a, w_ref[...].astype(BF16)) + b_ref[...]


def _ada_mod(c, w_ada, b_ada):
    B = c.shape[0]
    cp = jnp.zeros((SUBLANES, D_MODEL), F32).at[:B].set(c)
    n = 6 * D_MODEL
    out = pl.pallas_call(
        _ada_kernel,
        out_shape=jax.ShapeDtypeStruct((SUBLANES, n), F32),
        grid=(6,),
        in_specs=[
            pl.BlockSpec((SUBLANES, D_MODEL), lambda j: (0, 0)),
            pl.BlockSpec((D_MODEL, D_MODEL), lambda j: (0, j)),
            pl.BlockSpec((1, D_MODEL), lambda j: (0, j)),
        ],
        out_specs=pl.BlockSpec((SUBLANES, D_MODEL), lambda j: (0, j)),
        compiler_params=_params(("parallel",)),
        name="ada_mod",
    )(cp, w_ada, b_ada.reshape(1, n))
    return out[:B].reshape(B, 1, n)


def _in_proj_kernel(x_ref, mod_ref, n1_ref, w_ref, bg_ref, qn_ref, kvn_ref, wq_ref, wk_ref, wvt_ref,
                    cos_ref, sin_ref, one_ref,
                    q_ref, k_ref, vt_ref, qk_ref, vm_ref, om_ref, g_ref):
    x = x_ref[0]
    sh1 = mod_ref[0, :, 0:D_MODEL]
    sc1 = mod_ref[0, :, D_MODEL:2 * D_MODEL]
    h = (_rmsnorm(x, n1_ref[...]) * (1.0 + sc1) + sh1).astype(BF16)

    pa = _dot(h, w_ref[:, 0:_A_W])
    qk_ref[0] = _dot(h, w_ref[:, _QK0:_V0])
    vm_ref[0] = _dot(h, w_ref[:, _V0:_O0]).astype(BF16)
    om_ref[0] = _dot(h, w_ref[:, _O0:_G0])
    g_ref[0] = _dot(h, w_ref[:, _G0:_N_PREP]) + bg_ref[...]

    cos = cos_ref[...]
    sin = sin_ref[...]
    cos8 = jnp.concatenate([cos] * N_HEADS, axis=1)
    sin8 = jnp.concatenate([sin] * N_HEADS, axis=1)
    hw = N_HEADS * LANES

    cqn = _rmsnorm(pa[:, 0:Q_RANK], qn_ref[...]).astype(BF16)
    qq = _dot(cqn, wq_ref[...])
    q = (qq[:, 0:hw] * cos8 + qq[:, hw:2 * hw] * sin8).astype(BF16)

    ckvn = _rmsnorm(pa[:, Q_RANK:Q_RANK + KV_RANK], kvn_ref[...]).astype(BF16)
    kk = _dot(ckvn, wk_ref[...])
    kr0 = Q_RANK + KV_RANK
    kr = pa[:, kr0:kr0 + LANES] * cos + pa[:, kr0 + LANES:kr0 + 2 * LANES] * sin
    k = (kk + jnp.concatenate([kr] * N_HEADS, axis=1)).astype(BF16)
    vt = (_dot_nt(wvt_ref[...], ckvn) + one_ref[...]).astype(BF16)
    for hd in range(N_HEADS):
        sl = slice(hd * LANES, (hd + 1) * LANES)
        q_ref[0, hd] = q[:, sl]
        k_ref[0, hd] = k[:, sl]
        vt_ref[0, hd, 0] = vt[sl, :]


def _in_proj(x, mod, prep, tm):
    B, S, _ = x.shape
    grid = (B, S // tm)
    row = lambda b, i: (b, i, 0)
    hrow = lambda b, i: (b, 0, i, 0)
    head_shape = jax.ShapeDtypeStruct((B, N_HEADS, S, LANES), BF16)
    hw = N_HEADS * LANES
    return pl.pallas_call(
        _in_proj_kernel,
        out_shape=(
            head_shape, head_shape,
            jax.ShapeDtypeStruct((B, N_HEADS, S // tm, LANES, tm), BF16),
            jax.ShapeDtypeStruct((B, S, 2 * D_MLSTM), F32),
            jax.ShapeDtypeStruct((B, S, D_MLSTM), BF16),
            jax.ShapeDtypeStruct((B, S, D_MLSTM), F32),
            jax.ShapeDtypeStruct((B, S, LANES), F32),
        ),
        grid=grid,
        in_specs=[
            pl.BlockSpec((1, tm, D_MODEL), row),
            pl.BlockSpec((1, 1, 6 * D_MODEL), lambda b, i: (b, 0, 0)),
            _const_spec((1, D_MODEL)),
            _const_spec((D_MODEL, _N_PREP)),
            _const_spec((1, LANES)),
            _const_spec((1, Q_RANK)),
            _const_spec((1, KV_RANK)),
            _const_spec((Q_RANK, 2 * hw)),
            _const_spec((KV_RANK, hw)),
            _const_spec((hw, KV_RANK)),
            pl.BlockSpec((tm, LANES), lambda b, i: (i, 0)),
            pl.BlockSpec((tm, LANES), lambda b, i: (i, 0)),
            _const_spec((hw, 1)),
        ],
        out_specs=(
            pl.BlockSpec((1, N_HEADS, tm, LANES), hrow),
            pl.BlockSpec((1, N_HEADS, tm, LANES), hrow),
            pl.BlockSpec((1, N_HEADS, 1, LANES, tm), lambda b, i: (b, 0, i, 0, 0)),
            pl.BlockSpec((1, tm, 2 * D_MLSTM), row),
            pl.BlockSpec((1, tm, D_MLSTM), row),
            pl.BlockSpec((1, tm, D_MLSTM), row),
            pl.BlockSpec((1, tm, LANES), row),
        ),
        compiler_params=_params(("parallel", "parallel")),
        name="in_proj",
    )(x, mod, prep["norm1_g"], prep["w_in"], prep["b_gate"], prep["q_norm_g"], prep["kv_norm_g"],
      prep["w_q"], prep["w_k"], prep["w_vt"], prep["cos"], prep["sin"], prep["v_one"])


def _attn_kernel(q_ref, k_ref, vt_ref, o_ref, s_scr, acc_scr, *, sub):
    tq = q_ref.shape[2]
    ck = vt_ref.shape[4]
    tk = sub * ck
    n = k_ref.shape[2] // tk
    qt = q_ref[0, 0].astype(F32).T.astype(BF16)

    def scores(c, slot):
        off = pl.multiple_of(c * tk, tk)
        s = _dot(k_ref[0, 0, pl.ds(off, tk), :], qt)
        s_scr[slot] = s
        return jnp.max(s, axis=0, keepdims=True)

    def update(c, slot, mx, m):
        m_new = jnp.maximum(m, mx)
        alpha = jnp.exp2(m - m_new)
        p = jnp.exp2(s_scr[slot] - m_new).astype(BF16)
        pv = _dot(vt_ref[0, 0, c * sub], p[0:ck])
        for u in range(1, sub):
            pv = pv + _dot(vt_ref[0, 0, c * sub + u], p[u * ck:(u + 1) * ck])
        acc_scr[...] = alpha * acc_scr[...] + pv
        return m_new

    acc_scr[...] = jnp.zeros_like(acc_scr)
    mx0 = scores(0, 0)
    m0 = jnp.full((1, tq), -jnp.inf, F32)

    def body(jj, carry):
        m, mx_a = carry
        c = 2 * jj
        mx_b = scores(c + 1, 1)
        m = update(c, 0, mx_a, m)
        mx_a = scores(jnp.minimum(c + 2, n - 1), 0)
        m = update(c + 1, 1, mx_b, m)
        return m, mx_a

    lax.fori_loop(0, n // 2, body, (m0, mx0))
    acc = acc_scr[...]
    o_ref[0, 0] = (acc / acc[D_V:D_V + 1, :]).T.astype(BF16)


def _attention(q, k, vt, tq, sub):
    B, H, S, _ = q.shape
    ck = vt.shape[4]
    tk = sub * ck
    assert S % (2 * tk) == 0 and S % tq == 0
    return pl.pallas_call(
        functools.partial(_attn_kernel, sub=sub),
        out_shape=jax.ShapeDtypeStruct((B, H, S, LANES), BF16),
        grid=(B, H, S // tq),
        in_specs=[
            pl.BlockSpec((1, 1, tq, LANES), lambda b, h, i: (b, h, i, 0)),
            pl.BlockSpec((1, 1, S, LANES), lambda b, h, i: (b, h, 0, 0), pipeline_mode=pl.Buffered(1)),
            pl.BlockSpec((1, 1, S // ck, LANES, ck), lambda b, h, i: (b, h, 0, 0, 0),
                         pipeline_mode=pl.Buffered(1)),
        ],
        out_specs=pl.BlockSpec((1, 1, tq, LANES), lambda b, h, i: (b, h, i, 0)),
        scratch_shapes=[pltpu.VMEM((2, tk, tq), F32), pltpu.VMEM((LANES, tq), F32)],
        compiler_params=_params(("parallel", "parallel", "parallel")),
        name="attention",
    )(q, k, vt)


def _split3(x):
    hi = x.astype(BF16)
    r1 = x - hi.astype(F32)
    mid = r1.astype(BF16)
    lo = (r1 - mid.astype(F32)).astype(BF16)
    return hi, mid, lo


def _log_sigmoid(x):
    return jnp.minimum(x, 0.0) - jnp.log1p(jnp.exp(-jnp.abs(x)))


def _mlstm_direction(qk_ref, prev_ref, next_ref, v_ref, g_ref, cw_ref, cb_ref, h_ref,
                     c_scr, n_scr, m_scr, *, chunk_idx, n_chunks, reverse):
    L = CHUNK
    x = qk_ref[0]
    row = lax.broadcasted_iota(jnp.int32, (L, 1), 0)
    prev_row = jnp.where(chunk_idx > 0, prev_ref[0, SUBLANES - 1:SUBLANES, :], 0.0)
    next_row = jnp.where(chunk_idx < n_chunks - 1, next_ref[0, 0:1, :], 0.0)
    xm = jnp.where(row == 0, prev_row, pltpu.roll(x, 1, axis=0))
    xp = jnp.where(row == L - 1, next_row, pltpu.roll(x, L - 1, axis=0))
    conv = cw_ref[0:1, :] * xm + cw_ref[1:2, :] * x + cw_ref[2:3, :] * xp + cb_ref[...]
    qk = conv * jax.nn.sigmoid(conv)

    g = g_ref[0]
    gt = g.T
    ti = lax.broadcasted_iota(jnp.int32, (L, L), 0)
    si = lax.broadcasted_iota(jnp.int32, (L, L), 1)
    vis = (si >= ti) if reverse else (si <= ti)
    vis_bf = jnp.where(vis, 1.0, 0.0).astype(BF16)
    vis_t_bf = jnp.where((ti >= si) if reverse else (ti <= si), 1.0, 0.0).astype(BF16)
    ls = _log_sigmoid(g)
    ls_t = _log_sigmoid(gt)
    b_cols = sum(_dot(vis_bf, piece) for piece in _split3(ls))
    b_rows = sum(_dot(piece, vis_t_bf) for piece in _split3(ls_t))

    i_off = 2 * M_HEADS if reverse else 0
    f_off = i_off + M_HEADS
    last = 0 if reverse else L - 1
    scale = M_DIM ** -0.5
    for hd in range(M_HEADS):
        sl = slice(hd * M_DIM, (hd + 1) * M_DIM)
        q = qk[:, sl] * scale
        k = qk[:, D_MLSTM + hd * M_DIM:D_MLSTM + (hd + 1) * M_DIM]
        v = v_ref[0, :, sl]
        qb = q.astype(BF16)
        kb = k.astype(BF16)
        b_c = b_cols[:, f_off + hd:f_off + hd + 1]
        b_r = b_rows[f_off + hd:f_off + hd + 1, :]
        li_c = g[:, i_off + hd:i_off + hd + 1]
        li_r = gt[i_off + hd:i_off + hd + 1, :]
        m_prev = m_scr[hd]
        c_prev = c_scr[hd]
        n_prev = n_scr[hd]

        dmat = jnp.where(vis, b_c - b_r + li_r, -jnp.inf)
        inter = b_c + m_prev
        m_t = jnp.maximum(inter, jnp.max(dmat, axis=1, keepdims=True))
        w_intra = jnp.exp(dmat - m_t)
        w_state = jnp.exp(inter - m_t)
        s = _dot_nt(qb, kb) * w_intra
        num = _dot(s.astype(BF16), v) + w_state * _dot(qb, c_prev.astype(BF16))
        den = jnp.sum(s, axis=1, keepdims=True) + w_state * jnp.sum(q * n_prev, axis=1, keepdims=True)
        h_ref[0, :, sl] = num / jnp.maximum(jnp.abs(den), jnp.exp(-m_t))

        b_last = b_c[last:last + 1, :]
        a = b_last - b_c + li_c
        m_new = jnp.maximum(b_last + m_prev, jnp.max(a, axis=0, keepdims=True))
        w_s = jnp.exp(a - m_new)
        decay = jnp.exp(b_last + m_prev - m_new)
        kw = k * w_s
        c_scr[hd] = decay * c_prev + _dot_tn(kw.astype(BF16), v)
        n_scr[hd] = decay * n_prev + jnp.sum(kw, axis=0, keepdims=True)
        m_scr[hd] = m_new


def _mlstm_kernel(qkf_ref, pf_ref, nf_ref, vf_ref, gf_ref,
                  qkb_ref, pb_ref, nb_ref, vb_ref, gb_ref, cw_ref, cb_ref,
                  hf_ref, hb_ref, cf_scr, nf_scr, mf_scr, cb_scr, nb_scr, mb_scr):
    i = pl.program_id(1)
    nc = pl.num_programs(1)

    @pl.when(i == 0)
    def _():
        for r in (cf_scr, nf_scr, mf_scr, cb_scr, nb_scr, mb_scr):
            r[...] = jnp.zeros_like(r)

    _mlstm_direction(qkf_ref, pf_ref, nf_ref, vf_ref, gf_ref, cw_ref, cb_ref, hf_ref,
                     cf_scr, nf_scr, mf_scr, chunk_idx=i, n_chunks=nc, reverse=False)
    _mlstm_direction(qkb_ref, pb_ref, nb_ref, vb_ref, gb_ref, cw_ref, cb_ref, hb_ref,
                     cb_scr, nb_scr, mb_scr, chunk_idx=nc - 1 - i, n_chunks=nc, reverse=True)


def _mlstm(qk_raw, v_m, gates, conv_w, conv_b):
    B, S, _ = qk_raw.shape
    nc = S // CHUNK
    per = CHUNK // SUBLANES
    nsub = S // SUBLANES

    def specs(cidx):
        main = lambda b, i: (b, cidx(i), 0)
        prev = lambda b, i: (b, jnp.maximum(cidx(i) * per - 1, 0), 0)
        nxt = lambda b, i: (b, jnp.minimum((cidx(i) + 1) * per, nsub - 1), 0)
        return [
            pl.BlockSpec((1, CHUNK, 2 * D_MLSTM), main),
            pl.BlockSpec((1, SUBLANES, 2 * D_MLSTM), prev),
            pl.BlockSpec((1, SUBLANES, 2 * D_MLSTM), nxt),
            pl.BlockSpec((1, CHUNK, D_MLSTM), main),
            pl.BlockSpec((1, CHUNK, LANES), main),
        ]

    fwd = lambda i: i
    bwd = lambda i: nc - 1 - i
    state = [pltpu.VMEM((M_HEADS, M_DIM, M_DIM), F32), pltpu.VMEM((M_HEADS, 1, M_DIM), F32),
             pltpu.VMEM((M_HEADS, 1, 1), F32)]
    h_shape = jax.ShapeDtypeStruct((B, S, D_MLSTM), F32)
    return pl.pallas_call(
        _mlstm_kernel,
        out_shape=(h_shape, h_shape),
        grid=(B, nc),
        in_specs=specs(fwd) + specs(bwd) + [_const_spec((3, 2 * D_MLSTM)), _const_spec((1, 2 * D_MLSTM))],
        out_specs=(
            pl.BlockSpec((1, CHUNK, D_MLSTM), lambda b, i: (b, i, 0)),
            pl.BlockSpec((1, CHUNK, D_MLSTM), lambda b, i: (b, nc - 1 - i, 0)),
        ),
        scratch_shapes=state + state,
        compiler_params=_params(("parallel", "arbitrary")),
        name="mlstm",
    )(qk_raw, qk_raw, qk_raw, v_m, gates, qk_raw, qk_raw, qk_raw, v_m, gates, conv_w, conv_b)


def _out_proj_kernel(x_ref, mod_ref, a_ref, hf_ref, hb_ref, om_ref, mg_ref, wa_ref, wm_ref, o_ref):
    g1 = mod_ref[0, :, 2 * D_MODEL:3 * D_MODEL]
    attn = jnp.concatenate([a_ref[0, hd] for hd in range(N_HEADS)], axis=1)
    y = _dot(attn, wa_ref[...])
    h = hf_ref[0] + hb_ref[0]
    om = om_ref[0]
    parts = []
    for hd in range(M_HEADS):
        sl = slice(hd * M_DIM, (hd + 1) * M_DIM)
        hh = h[:, sl]
        mu = jnp.mean(hh, axis=1, keepdims=True)
        d = hh - mu
        var = jnp.mean(d * d, axis=1, keepdims=True)
        parts.append(d * lax.rsqrt(var + EPS) * mg_ref[:, sl] * jax.nn.sigmoid(om[:, sl]))
    mem = jnp.concatenate(parts, axis=1).astype(BF16)
    y = y + _dot(mem, wm_ref[...])
    o_ref[0] = x_ref[0] + g1 * y


def _out_proj(x, mod, attn, h_f, h_b, o_m, prep, tm):
    B, S, _ = x.shape
    row = lambda b, i: (b, i, 0)
    return pl.pallas_call(
        _out_proj_kernel,
        out_shape=jax.ShapeDtypeStruct((B, S, D_MODEL), F32),
        grid=(B, S // tm),
        in_specs=[
            pl.BlockSpec((1, tm, D_MODEL), row),
            pl.BlockSpec((1, 1, 6 * D_MODEL), lambda b, i: (b, 0, 0)),
            pl.BlockSpec((1, N_HEADS, tm, LANES), lambda b, i: (b, 0, i, 0)),
            pl.BlockSpec((1, tm, D_MLSTM), row),
            pl.BlockSpec((1, tm, D_MLSTM), row),
            pl.BlockSpec((1, tm, D_MLSTM), row),
            _const_spec((1, D_MLSTM)),
            _const_spec((N_HEADS * LANES, D_MODEL)),
            _const_spec((D_MLSTM, D_MODEL)),
        ],
        out_specs=pl.BlockSpec((1, tm, D_MODEL), row),
        compiler_params=_params(("parallel", "parallel")),
        name="out_proj",
    )(x, mod, attn, h_f, h_b, o_m, prep["mh_norm_g"], prep["w_out_attn"], prep["w_out_mem"])


def _ffn_kernel(x_ref, xp_ref, xn_ref, mod_ref, n2_ref, wu_ref, cw_ref, cb_ref, wd_ref, fg_ref, o_ref):
    i = pl.program_id(1)
    nt = pl.num_programs(1)
    tm = x_ref.shape[1]
    rows = tm + 2 * SUBLANES
    sh2 = mod_ref[0, :, 3 * D_MODEL:4 * D_MODEL]
    sc2 = mod_ref[0, :, 4 * D_MODEL:5 * D_MODEL]
    g2 = mod_ref[0, :, 5 * D_MODEL:6 * D_MODEL]

    def norm_mod(r):
        return _rmsnorm(r, n2_ref[...]) * (1.0 + sc2) + sh2

    x = x_ref[0]
    hp = jnp.where(i > 0, norm_mod(xp_ref[0]), 0.0)
    hn = jnp.where(i < nt - 1, norm_mod(xn_ref[0]), 0.0)
    hext = jnp.concatenate([hp, norm_mod(x), hn], axis=0).astype(BF16)

    def conv(u, c0):
        cs = slice(c0, c0 + FF_CHUNK)
        um = pltpu.roll(u, 1, axis=0)
        up = pltpu.roll(u, rows - 1, axis=0)
        full = cw_ref[0:1, cs] * um + cw_ref[1:2, cs] * u + cw_ref[2:3, cs] * up + cb_ref[:, cs]
        return full[SUBLANES:SUBLANES + tm]

    acc = jnp.zeros((tm, D_MODEL), F32)
    for j in range(D_FF // FF_CHUNK):
        c0 = j * FF_CHUNK
        a = conv(_dot(hext, wu_ref[:, c0:c0 + FF_CHUNK]), c0)
        g = conv(_dot(hext, wu_ref[:, D_FF + c0:D_FF + c0 + FF_CHUNK]), D_FF + c0)
        act = (g * jax.nn.sigmoid(g) * a).astype(BF16)
        acc = acc + _dot(act, wd_ref[c0:c0 + FF_CHUNK, :])
    o_ref[0] = _rmsnorm(x + g2 * acc, fg_ref[...])


def _conv_ffn(x, mod, prep, tm):
    B, S, _ = x.shape
    per = tm // SUBLANES
    nsub = S // SUBLANES
    row = lambda b, i: (b, i, 0)
    return pl.pallas_call(
        _ffn_kernel,
        out_shape=jax.ShapeDtypeStruct((B, S, D_MODEL), F32),
        grid=(B, S // tm),
        in_specs=[
            pl.BlockSpec((1, tm, D_MODEL), row),
            pl.BlockSpec((1, SUBLANES, D_MODEL), lambda b, i: (b, jnp.maximum(i * per - 1, 0), 0)),
            pl.BlockSpec((1, SUBLANES, D_MODEL), lambda b, i: (b, jnp.minimum((i + 1) * per, nsub - 1), 0)),
            pl.BlockSpec((1, 1, 6 * D_MODEL), lambda b, i: (b, 0, 0)),
            _const_spec((1, D_MODEL)),
            _const_spec((D_MODEL, 2 * D_FF)),
            _const_spec((3, 2 * D_FF)),
            _const_spec((1, 2 * D_FF)),
            _const_spec((D_FF, D_MODEL)),
            _const_spec((1, D_MODEL)),
        ],
        out_specs=pl.BlockSpec((1, tm, D_MODEL), row),
        compiler_params=_params(("parallel", "parallel")),
        name="conv_ffn",
    )(x, x, x, mod, prep["norm2_g"], prep["w_up"], prep["conv_f_w"], prep["conv_f_b"],
      prep["w_down"], prep["final_g"])


def _head_groups(w, n_heads, width, pieces):
    rows = w.shape[0]
    wh = w.reshape(rows, n_heads, width)
    out = jnp.zeros((rows, n_heads, LANES), F32)
    for src, n, dst, sign in pieces:
        out = out.at[:, :, dst:dst + n].set(sign * wh[:, :, src:src + n])
    return out.reshape(rows, n_heads * LANES)


def _prepare(S, norm1_g, w_in, b_gate, q_norm_g, kv_norm_g, w_uq, w_ukv, conv_m_w, conv_m_b,
             mh_norm_g, w_out, norm2_g, w_up, conv_f_w, conv_f_b, w_down, final_g):
    half = D_ROPE // 2
    r1 = D_NOPE
    r2 = D_NOPE + half
    c0, c1, c2 = Q_RANK, Q_RANK + KV_RANK, Q_RANK + KV_RANK + D_ROPE
    c3 = c2 + 2 * D_MLSTM
    c4 = c3 + D_MLSTM
    c5 = c4 + D_MLSTM

    kr = w_in[:, c1:c2]
    zeros = lambda n: jnp.zeros((D_MODEL, n), F32)
    kr_main = jnp.concatenate([zeros(r1), kr[:, :half], kr[:, half:], zeros(LANES - r2 - half)], axis=1)
    kr_swap = jnp.concatenate([zeros(r1), -kr[:, half:], kr[:, :half], zeros(LANES - r2 - half)], axis=1)
    gates = jnp.concatenate([w_in[:, c5:], zeros(LANES - 4 * M_HEADS)], axis=1)
    w_prep = jnp.concatenate([w_in[:, :c1], kr_main, kr_swap, w_in[:, c2:c5], gates], axis=1)

    qs = (D_NOPE + D_ROPE) ** -0.5 * math.log2(math.e)
    dq = D_NOPE + D_ROPE
    wq_main = _head_groups(w_uq, N_HEADS, dq, [(0, D_NOPE, 0, 1.0), (D_NOPE, half, r1, 1.0),
                                                (D_NOPE + half, half, r2, 1.0)])
    wq_swap = _head_groups(w_uq, N_HEADS, dq, [(D_NOPE + half, half, r1, -1.0), (D_NOPE, half, r2, 1.0)])
    w_q = jnp.concatenate([wq_main, wq_swap], axis=1) * qs

    dkv = D_NOPE + D_V
    wk = _head_groups(w_ukv, N_HEADS, dkv, [(0, D_NOPE, 0, 1.0)])
    wv = _head_groups(w_ukv, N_HEADS, dkv, [(D_NOPE, D_V, 0, 1.0)])

    inv = 1.0 / (ROPE_THETA ** (jnp.arange(half, dtype=F32) * (2.0 / D_ROPE)))
    ang = jnp.arange(S, dtype=jnp.int32).astype(F32)[:, None] * inv[None, :]
    ones = jnp.ones((S, r1), F32)
    pad = jnp.zeros((S, LANES - r2 - half), F32)
    cos = jnp.concatenate([ones, jnp.cos(ang), jnp.cos(ang), pad], axis=1)
    sin = jnp.concatenate([0.0 * ones, jnp.sin(ang), jnp.sin(ang), pad], axis=1)

    v_one = jnp.zeros((N_HEADS, LANES), F32).at[:, D_V].set(1.0).reshape(N_HEADS * LANES, 1)
    b_g = jnp.concatenate([b_gate, jnp.zeros((LANES - 4 * M_HEADS,), F32)]).reshape(1, LANES)

    d_attn = N_HEADS * D_V
    w_oa = jnp.zeros((N_HEADS, LANES, D_MODEL), F32).at[:, :D_V, :].set(
        w_out[:d_attn].reshape(N_HEADS, D_V, D_MODEL)).reshape(N_HEADS * LANES, D_MODEL)

    return dict(
        norm1_g=norm1_g.reshape(1, -1), w_in=w_prep.astype(BF16), b_gate=b_g,
        q_norm_g=q_norm_g.reshape(1, -1), kv_norm_g=kv_norm_g.reshape(1, -1),
        w_q=w_q.astype(BF16), w_k=wk.astype(BF16), w_vt=wv.T.astype(BF16), cos=cos, sin=sin, v_one=v_one,
        conv_m_w=conv_m_w, conv_m_b=conv_m_b.reshape(1, -1), mh_norm_g=mh_norm_g.reshape(1, -1),
        w_out_attn=w_oa.astype(BF16), w_out_mem=w_out[d_attn:].astype(BF16),
        norm2_g=norm2_g.reshape(1, -1), w_up=w_up.astype(BF16), conv_f_w=conv_f_w,
        conv_f_b=conv_f_b.reshape(1, -1), w_down=w_down.astype(BF16), final_g=final_g.reshape(1, -1),
    )


def _tiles(S):
    tm = min(512, S // 2)
    sub = 2 if S % (4 * tm) == 0 else 1
    return tm, min(512, S), sub


def _trunk(x, c, w_ada, b_ada, prep):
    S = x.shape[1]
    tm, tq, sub = _tiles(S)
    mod = _ada_mod(c, w_ada, b_ada)
    q, k, vt, qk_raw, v_m, o_m, gates = _in_proj(x, mod, prep, tm)
    attn = _attention(q, k, vt, tq, sub)
    h_f, h_b = _mlstm(qk_raw, v_m, gates, prep["conv_m_w"], prep["conv_m_b"])
    x1 = _out_proj(x, mod, attn, h_f, h_b, o_m, prep, tm)
    return _conv_ffn(x1, mod, prep, tm)


def kernel(x_prompt, x_sample, c_prompt, c_sample, norm1_g, w_ada, b_ada, w_in, b_gate, q_norm_g, kv_norm_g, w_uq, w_ukv, conv_m_w, conv_m_b, mh_norm_g, w_out, norm2_g, w_up, conv_f_w, conv_f_b, w_down, final_g):
    assert w_ada.shape[0] == 1, "single-layer trunk"
    assert x_prompt.shape[1] == x_sample.shape[1]
    prep = _prepare(x_prompt.shape[1], norm1_g[0], w_in[0], b_gate[0], q_norm_g[0], kv_norm_g[0], w_uq[0],
                    w_ukv[0], conv_m_w[0], conv_m_b[0], mh_norm_g[0], w_out[0], norm2_g[0], w_up[0],
                    conv_f_w[0], conv_f_b[0], w_down[0], final_g)
    y_prompt = _trunk(x_prompt, c_prompt, w_ada[0], b_ada[0], prep)
    y_sample = _trunk(x_sample, c_sample, w_ada[0], b_ada[0], prep)
    return (y_prompt, y_sample)
```

```python
import functools
import math

import jax
import jax.numpy as jnp
from jax import lax
from jax.experimental import pallas as pl
from jax.experimental.pallas import tpu as pltpu

F32 = jnp.float32
BF16 = jnp.bfloat16

D_MODEL = 1024
N_HEADS = 8
D_NOPE = 64
D_ROPE = 32
D_V = 64
Q_RANK = 384
KV_RANK = 256
M_HEADS = 4
M_DIM = 128
D_MLSTM = M_HEADS * M_DIM
CHUNK = 128
M_CPS = 4
D_FF = 2816
ROPE_THETA = 10000.0
EPS = 1e-6
LANES = 128
SUBLANES = 8
FF_CHUNK = 256
ST_ROWS = M_DIM + 16
PV_ROWS = 80
VMEM_LIMIT = 56 * 1024 * 1024

_A_W = Q_RANK + KV_RANK + 2 * LANES
_QK0 = _A_W
_O0 = _QK0 + 2 * D_MLSTM
_G0 = _O0 + D_MLSTM
_N_PREP = _G0 + LANES


def _dot(a, b):
    return jnp.dot(a, b, preferred_element_type=F32)


def _dot_nt(a, b):
    return lax.dot_general(a, b, (((1,), (1,)), ((), ())), preferred_element_type=F32)


def _dot_tn(a, b):
    return lax.dot_general(a, b, (((0,), (0,)), ((), ())), preferred_element_type=F32)


def _rmsnorm(x, g):
    return x * lax.rsqrt(jnp.mean(x * x, axis=-1, keepdims=True) + EPS) * g


def _log_sigmoid(x):
    return jnp.minimum(x, 0.0) - jnp.log1p(jnp.exp(-jnp.abs(x)))


def _params(sem):
    return pltpu.CompilerParams(dimension_semantics=sem, vmem_limit_bytes=VMEM_LIMIT)


def _const_spec(shape):
    nd = len(shape)
    return pl.BlockSpec(shape, lambda *_: (0,) * nd, pipeline_mode=pl.Buffered(1))


def _halo_specs(tm, S, width):
    per = tm // SUBLANES
    nsub = S // SUBLANES
    return [
        pl.BlockSpec((1, SUBLANES, width), lambda b, i: (b, jnp.maximum(i * per - 1, 0), 0)),
        pl.BlockSpec((1, SUBLANES, width), lambda b, i: (b, jnp.minimum((i + 1) * per, nsub - 1), 0)),
    ]


def _conv3(u, cw_ref, cb_ref, cols, tm):
    rows = tm + 2 * SUBLANES
    full = (cw_ref[0:1, cols] * pltpu.roll(u, 1, axis=0) + cw_ref[1:2, cols] * u
            + cw_ref[2:3, cols] * pltpu.roll(u, rows - 1, axis=0) + cb_ref[:, cols])
    return full[SUBLANES:SUBLANES + tm]


def _ada_kernel(c_ref, w_ref, b_ref, o_ref):
    c = c_ref[...]
    a = (c * jax.nn.sigmoid(c)).astype(BF16)
    o_ref[...] = _dot(a, w_ref[...].astype(BF16)) + b_ref[...]


def _ada_mod(c, w_ada, b_ada):
    B = c.shape[0]
    cp = jnp.zeros((SUBLANES, D_MODEL), F32).at[:B].set(c)
    n = 6 * D_MODEL
    out = pl.pallas_call(
        _ada_kernel,
        out_shape=jax.ShapeDtypeStruct((SUBLANES, n), F32),
        grid=(6,),
        in_specs=[
            pl.BlockSpec((SUBLANES, D_MODEL), lambda j: (0, 0)),
            pl.BlockSpec((D_MODEL, D_MODEL), lambda j: (0, j)),
            pl.BlockSpec((1, D_MODEL), lambda j: (0, j)),
        ],
        out_specs=pl.BlockSpec((SUBLANES, D_MODEL), lambda j: (0, j)),
        compiler_params=_params(("parallel",)),
        name="ada_mod",
    )(cp, w_ada, b_ada.reshape(1, n))
    return out[:B].reshape(B, 1, n)


def _in_proj_kernel(x_ref, xp_ref, xn_ref, mod_ref, n1_ref, w_ref, bg_ref, qn_ref, kvn_ref, wq_ref, wk_ref,
                    wvt_ref, cos_ref, sin_ref, one_ref, cw_ref, cb_ref, wvm_ref,
                    q_ref, k_ref, vt_ref, qk_ref, vm_ref, om_ref, g_ref):
    i = pl.program_id(1)
    nt = pl.num_programs(1)
    tm = x_ref.shape[1]
    sh1 = mod_ref[0, :, 0:D_MODEL]
    sc1 = mod_ref[0, :, D_MODEL:2 * D_MODEL]

    def norm_mod(r):
        return _rmsnorm(r, n1_ref[...]) * (1.0 + sc1) + sh1

    hf = norm_mod(x_ref[0])
    h = hf.astype(BF16)
    hp = jnp.where(i > 0, norm_mod(xp_ref[0]), 0.0)
    hn = jnp.where(i < nt - 1, norm_mod(xn_ref[0]), 0.0)
    hext = jnp.concatenate([hp, hf, hn], axis=0).astype(BF16)
    conv = _conv3(_dot(hext, w_ref[:, _QK0:_O0]), cw_ref, cb_ref, slice(None), tm)
    qk = conv * jax.nn.sigmoid(conv)
    qk_ref[0, :, 0:D_MLSTM] = (qk[:, 0:D_MLSTM] * (M_DIM ** -0.5)).astype(BF16)
    qk_ref[0, :, D_MLSTM:] = qk[:, D_MLSTM:].astype(BF16)

    pa = _dot(h, w_ref[:, 0:_A_W])
    vm_ref[0] = _dot_nt(wvm_ref[...], h).astype(BF16)
    om_ref[0] = _dot(h, w_ref[:, _O0:_G0])
    g = _dot(h, w_ref[:, _G0:_N_PREP]) + bg_ref[...]
    lane = lax.broadcasted_iota(jnp.int32, g.shape, 1)
    is_forget = (lane < 4 * M_HEADS) & ((lane // M_HEADS) % 2 == 1)
    g_ref[0] = jnp.where(is_forget, _log_sigmoid(g), g)

    cos = cos_ref[...]
    sin = sin_ref[...]
    cos8 = jnp.concatenate([cos] * N_HEADS, axis=1)
    sin8 = jnp.concatenate([sin] * N_HEADS, axis=1)
    hw = N_HEADS * LANES

    cqn = _rmsnorm(pa[:, 0:Q_RANK], qn_ref[...]).astype(BF16)
    qq = _dot(cqn, wq_ref[...])
    q = (qq[:, 0:hw] * cos8 + qq[:, hw:2 * hw] * sin8).astype(BF16)

    ckvn = _rmsnorm(pa[:, Q_RANK:Q_RANK + KV_RANK], kvn_ref[...]).astype(BF16)
    kk = _dot(ckvn, wk_ref[...])
    kr0 = Q_RANK + KV_RANK
    kr = pa[:, kr0:kr0 + LANES] * cos + pa[:, kr0 + LANES:kr0 + 2 * LANES] * sin
    k = (kk + jnp.concatenate([kr] * N_HEADS, axis=1)).astype(BF16)
    vt = (_dot_nt(wvt_ref[...], ckvn) + one_ref[...]).astype(BF16)
    for hd in range(N_HEADS):
        sl = slice(hd * LANES, (hd + 1) * LANES)
        q_ref[0, hd] = q[:, sl]
        k_ref[0, hd] = k[:, sl]
        vt_ref[0, hd, 0] = vt[sl, :]


def _in_proj(x, mod, prep, tm):
    B, S, _ = x.shape
    grid = (B, S // tm)
    row = lambda b, i: (b, i, 0)
    hrow = lambda b, i: (b, 0, i, 0)
    head_shape = jax.ShapeDtypeStruct((B, N_HEADS, S, LANES), BF16)
    hw = N_HEADS * LANES
    return pl.pallas_call(
        _in_proj_kernel,
        out_shape=(
            head_shape, head_shape,
            jax.ShapeDtypeStruct((B, N_HEADS, S // tm, LANES, tm), BF16),
            jax.ShapeDtypeStruct((B, S, 2 * D_MLSTM), BF16),
            jax.ShapeDtypeStruct((B, D_MLSTM, S), BF16),
            jax.ShapeDtypeStruct((B, S, D_MLSTM), F32),
            jax.ShapeDtypeStruct((B, S, LANES), F32),
        ),
        grid=grid,
        in_specs=[pl.BlockSpec((1, tm, D_MODEL), row)] + _halo_specs(tm, S, D_MODEL) + [
            pl.BlockSpec((1, 1, 6 * D_MODEL), lambda b, i: (b, 0, 0)),
            _const_spec((1, D_MODEL)),
            _const_spec((D_MODEL, _N_PREP)),
            _const_spec((1, LANES)),
            _const_spec((1, Q_RANK)),
            _const_spec((1, KV_RANK)),
            _const_spec((Q_RANK, 2 * hw)),
            _const_spec((KV_RANK, hw)),
            _const_spec((hw, KV_RANK)),
            pl.BlockSpec((tm, LANES), lambda b, i: (i, 0)),
            pl.BlockSpec((tm, LANES), lambda b, i: (i, 0)),
            _const_spec((hw, 1)),
            _const_spec((3, 2 * D_MLSTM)),
            _const_spec((1, 2 * D_MLSTM)),
            _const_spec((D_MLSTM, D_MODEL)),
        ],
        out_specs=(
            pl.BlockSpec((1, N_HEADS, tm, LANES), hrow),
            pl.BlockSpec((1, N_HEADS, tm, LANES), hrow),
            pl.BlockSpec((1, N_HEADS, 1, LANES, tm), lambda b, i: (b, 0, i, 0, 0)),
            pl.BlockSpec((1, tm, 2 * D_MLSTM), row),
            pl.BlockSpec((1, D_MLSTM, tm), lambda b, i: (b, 0, i)),
            pl.BlockSpec((1, tm, D_MLSTM), row),
            pl.BlockSpec((1, tm, LANES), row),
        ),
        compiler_params=_params(("parallel", "parallel")),
        name="in_proj",
    )(x, x, x, mod, prep["norm1_g"], prep["w_in"], prep["b_gate"], prep["q_norm_g"], prep["kv_norm_g"],
      prep["w_q"], prep["w_k"], prep["w_vt"], prep["cos"], prep["sin"], prep["v_one"],
      prep["conv_m_w"], prep["conv_m_b"], prep["w_vm_t"])


def _attn_kernel(q_ref, qn_ref, k_ref, vt_ref, o_ref, s_scr, mx_scr, *, sub):
    i = pl.program_id(2)
    ck = vt_ref.shape[4]
    tk = sub * ck
    n = k_ref.shape[2] // tk

    def scores(qt, c, slot):
        s = _dot(k_ref[0, 0, c * tk:(c + 1) * tk, :], qt)
        s_scr[slot] = s
        return jnp.max(s, axis=0, keepdims=True)

    qt = q_ref[0, 0].astype(F32).T.astype(BF16)

    @pl.when(i == 0)
    def _():
        mx_scr[...] = scores(qt, 0, 0)

    mx = mx_scr[...]
    m = acc = None
    for c in range(n):
        if c + 1 < n:
            mx_next = scores(qt, c + 1, (c + 1) % 2)
        else:
            mx_next = scores(qn_ref[0, 0].astype(F32).T.astype(BF16), 0, 0)
        m_new = mx if c == 0 else jnp.maximum(m, mx)
        p = jnp.exp2(s_scr[c % 2] - m_new).astype(BF16)
        pv = _dot(vt_ref[0, 0, c * sub, 0:PV_ROWS, :], p[0:ck])
        for u in range(1, sub):
            pv = pv + _dot(vt_ref[0, 0, c * sub + u, 0:PV_ROWS, :], p[u * ck:(u + 1) * ck])
        acc = pv if c == 0 else jnp.exp2(m - m_new) * acc + pv
        m, mx = m_new, mx_next
    mx_scr[...] = mx
    out = acc[0:D_V] / acc[D_V:D_V + 1]
    o_ref[0, 0] = jnp.concatenate([out, jnp.zeros((LANES - D_V, out.shape[1]), F32)], axis=0).T.astype(BF16)


def _attention(q, k, vt, tq, sub):
    B, H, S, _ = q.shape
    ck = vt.shape[4]
    tk = sub * ck
    nq = S // tq
    assert S % (2 * tk) == 0 and S % tq == 0
    return pl.pallas_call(
        functools.partial(_attn_kernel, sub=sub),
        out_shape=jax.ShapeDtypeStruct((B, H, S, LANES), BF16),
        grid=(B, H, nq),
        in_specs=[
            pl.BlockSpec((1, 1, tq, LANES), lambda b, h, i: (b, h, i, 0)),
            pl.BlockSpec((1, 1, tq, LANES), lambda b, h, i: (b, h, jnp.minimum(i + 1, nq - 1), 0)),
            pl.BlockSpec((1, 1, S, LANES), lambda b, h, i: (b, h, 0, 0), pipeline_mode=pl.Buffered(1)),
            pl.BlockSpec((1, 1, S // ck, LANES, ck), lambda b, h, i: (b, h, 0, 0, 0),
                         pipeline_mode=pl.Buffered(1)),
        ],
        out_specs=pl.BlockSpec((1, 1, tq, LANES), lambda b, h, i: (b, h, i, 0)),
        scratch_shapes=[pltpu.VMEM((2, tk, tq), F32), pltpu.VMEM((1, tq), F32)],
        compiler_params=_params(("parallel", "parallel", "arbitrary")),
        name="attention",
    )(q, q, k, vt)


def _split3(x):
    hi = x.astype(BF16)
    r1 = x - hi.astype(F32)
    mid = r1.astype(BF16)
    lo = (r1 - mid.astype(F32)).astype(BF16)
    return hi, mid, lo


def _mlstm_direction(qk_ref, vt_ref, g_ref, h_ref, c_scr, m_scr, *, sub, reverse):
    L = CHUNK
    rs = slice(sub * L, (sub + 1) * L)
    g = g_ref[0, rs, :]
    gt = g.T
    si = lax.broadcasted_iota(jnp.int32, (L, L), 0)
    ti = lax.broadcasted_iota(jnp.int32, (L, L), 1)
    vis = (si >= ti) if reverse else (si <= ti)
    vis_bf = jnp.where(vis, 1.0, 0.0).astype(BF16)
    vis_t_bf = jnp.where((ti >= si) if reverse else (ti <= si), 1.0, 0.0).astype(BF16)
    b_cols = sum(_dot(vis_t_bf, piece) for piece in _split3(g))
    b_rows = sum(_dot(piece, vis_bf) for piece in _split3(gt))
    one_rows = jnp.where(lax.broadcasted_iota(jnp.int32, (ST_ROWS - M_DIM, L), 0) == 0, 1.0, 0.0).astype(BF16)

    i_off = 2 * M_HEADS if reverse else 0
    f_off = i_off + M_HEADS
    last = 0 if reverse else L - 1
    for hd in range(M_HEADS):
        sl = slice(hd * M_DIM, (hd + 1) * M_DIM)
        qb = qk_ref[0, rs, sl]
        kb = qk_ref[0, rs, D_MLSTM + hd * M_DIM:D_MLSTM + (hd + 1) * M_DIM]
        vt_ext = jnp.concatenate([vt_ref[0, sl, rs], one_rows], axis=0)
        b_r = b_rows[f_off + hd:f_off + hd + 1, :]
        li_r = gt[i_off + hd:i_off + hd + 1, :]
        u_c = g[:, i_off + hd:i_off + hd + 1] - b_cols[:, f_off + hd:f_off + hd + 1]
        m_prev = m_scr[hd]
        c_prev = c_scr[hd]

        dmat = jnp.where(vis, b_r + u_c, -jnp.inf)
        inter = b_r + m_prev
        m_t = jnp.maximum(inter, jnp.max(dmat, axis=0, keepdims=True))
        w_intra = jnp.exp(dmat - m_t)
        w_state = jnp.exp(inter - m_t)
        s = _dot_nt(kb, qb) * w_intra
        tot = _dot(vt_ext, s.astype(BF16)) + w_state * _dot_nt(c_prev.astype(BF16), qb)
        den = tot[M_DIM:M_DIM + 1, :]
        h_t = tot[0:M_DIM] / jnp.maximum(jnp.abs(den), jnp.exp(-m_t))
        h_ref[0, rs, sl] = h_t.T

        b_last = b_r[:, last:last + 1]
        a = b_last - b_r + li_r
        m_new = jnp.maximum(b_last + m_prev, jnp.max(a, axis=1, keepdims=True))
        w_s = jnp.exp(a - m_new)
        decay = jnp.exp(b_last + m_prev - m_new)
        vw = (vt_ext.astype(F32) * w_s).astype(BF16)
        c_scr[hd] = decay * c_prev + _dot(vw, kb)
        m_scr[hd] = m_new


def _mlstm_kernel(qkf_ref, vf_ref, gf_ref, qkb_ref, vb_ref, gb_ref,
                  hf_ref, hb_ref, cf_scr, mf_scr, cb_scr, mb_scr):
    @pl.when(pl.program_id(1) == 0)
    def _():
        for r in (cf_scr, mf_scr, cb_scr, mb_scr):
            r[...] = jnp.zeros_like(r)

    for j in range(M_CPS):
        _mlstm_direction(qkf_ref, vf_ref, gf_ref, hf_ref, cf_scr, mf_scr, sub=j, reverse=False)
        _mlstm_direction(qkb_ref, vb_ref, gb_ref, hb_ref, cb_scr, mb_scr, sub=M_CPS - 1 - j, reverse=True)


def _mlstm(qk_act, v_m, gates):
    B, S, _ = qk_act.shape
    rows = M_CPS * CHUNK
    nc = S // rows

    def specs(cidx):
        main = lambda b, i: (b, cidx(i), 0)
        return [
            pl.BlockSpec((1, rows, 2 * D_MLSTM), main),
            pl.BlockSpec((1, D_MLSTM, rows), lambda b, i: (b, 0, cidx(i))),
            pl.BlockSpec((1, rows, LANES), main),
        ]

    fwd = lambda i: i
    bwd = lambda i: nc - 1 - i
    state = [pltpu.VMEM((M_HEADS, ST_ROWS, M_DIM), F32), pltpu.VMEM((M_HEADS, 1, 1), F32)]
    h_shape = jax.ShapeDtypeStruct((B, S, D_MLSTM), F32)
    return pl.pallas_call(
        _mlstm_kernel,
        out_shape=(h_shape, h_shape),
        grid=(B, nc),
        in_specs=specs(fwd) + specs(bwd),
        out_specs=(
            pl.BlockSpec((1, rows, D_MLSTM), lambda b, i: (b, i, 0)),
            pl.BlockSpec((1, rows, D_MLSTM), lambda b, i: (b, nc - 1 - i, 0)),
        ),
        scratch_shapes=state + state,
        compiler_params=_params(("parallel", "arbitrary")),
        name="mlstm",
    )(qk_act, v_m, gates, qk_act, v_m, gates)


def _out_proj_kernel(x_ref, mod_ref, a_ref, hf_ref, hb_ref, om_ref, mg_ref, wa_ref, wm_ref, o_ref):
    g1 = mod_ref[0, :, 2 * D_MODEL:3 * D_MODEL]
    attn = jnp.concatenate([a_ref[0, hd] for hd in range(N_HEADS)], axis=1)
    y = _dot(attn, wa_ref[...])
    h = hf_ref[0] + hb_ref[0]
    om = om_ref[0]
    parts = []
    for hd in range(M_HEADS):
        sl = slice(hd * M_DIM, (hd + 1) * M_DIM)
        hh = h[:, sl]
        mu = jnp.mean(hh, axis=1, keepdims=True)
        d = hh - mu
        var = jnp.mean(d * d, axis=1, keepdims=True)
        parts.append(d * lax.rsqrt(var + EPS) * mg_ref[:, sl] * jax.nn.sigmoid(om[:, sl]))
    mem = jnp.concatenate(parts, axis=1).astype(BF16)
    y = y + _dot(mem, wm_ref[...])
    o_ref[0] = x_ref[0] + g1 * y


def _out_proj(x, mod, attn, h_f, h_b, o_m, prep, tm):
    B, S, _ = x.shape
    row = lambda b, i: (b, i, 0)
    return pl.pallas_call(
        _out_proj_kernel,
        out_shape=jax.ShapeDtypeStruct((B, S, D_MODEL), F32),
        grid=(B, S // tm),
        in_specs=[
            pl.BlockSpec((1, tm, D_MODEL), row),
            pl.BlockSpec((1, 1, 6 * D_MODEL), lambda b, i: (b, 0, 0)),
            pl.BlockSpec((1, N_HEADS, tm, LANES), lambda b, i: (b, 0, i, 0)),
            pl.BlockSpec((1, tm, D_MLSTM), row),
            pl.BlockSpec((1, tm, D_MLSTM), row),
            pl.BlockSpec((1, tm, D_MLSTM), row),
            _const_spec((1, D_MLSTM)),
            _const_spec((N_HEADS * LANES, D_MODEL)),
            _const_spec((D_MLSTM, D_MODEL)),
        ],
        out_specs=pl.BlockSpec((1, tm, D_MODEL), row),
        compiler_params=_params(("parallel", "parallel")),
        name="out_proj",
    )(x, mod, attn, h_f, h_b, o_m, prep["mh_norm_g"], prep["w_out_attn"], prep["w_out_mem"])


def _ffn_kernel(x_ref, xp_ref, xn_ref, mod_ref, n2_ref, wu_ref, cw_ref, cb_ref, wd_ref, fg_ref, o_ref):
    i = pl.program_id(1)
    nt = pl.num_programs(1)
    tm = x_ref.shape[1]
    sh2 = mod_ref[0, :, 3 * D_MODEL:4 * D_MODEL]
    sc2 = mod_ref[0, :, 4 * D_MODEL:5 * D_MODEL]
    g2 = mod_ref[0, :, 5 * D_MODEL:6 * D_MODEL]

    def norm_mod(r):
        return _rmsnorm(r, n2_ref[...]) * (1.0 + sc2) + sh2

    x = x_ref[0]
    hp = jnp.where(i > 0, norm_mod(xp_ref[0]), 0.0)
    hn = jnp.where(i < nt - 1, norm_mod(xn_ref[0]), 0.0)
    hext = jnp.concatenate([hp, norm_mod(x), hn], axis=0).astype(BF16)

    acc = jnp.zeros((tm, D_MODEL), F32)
    for j in range(D_FF // FF_CHUNK):
        ca = slice(j * FF_CHUNK, (j + 1) * FF_CHUNK)
        cg = slice(D_FF + j * FF_CHUNK, D_FF + (j + 1) * FF_CHUNK)
        a = _conv3(_dot(hext, wu_ref[:, ca]), cw_ref, cb_ref, ca, tm)
        g = _conv3(_dot(hext, wu_ref[:, cg]), cw_ref, cb_ref, cg, tm)
        act = (g * jax.nn.sigmoid(g) * a).astype(BF16)
        acc = acc + _dot(act, wd_ref[ca, :])
    o_ref[0] = _rmsnorm(x + g2 * acc, fg_ref[...])


def _conv_ffn(x, mod, prep, tm):
    B, S, _ = x.shape
    row = lambda b, i: (b, i, 0)
    return pl.pallas_call(
        _ffn_kernel,
        out_shape=jax.ShapeDtypeStruct((B, S, D_MODEL), F32),
        grid=(B, S // tm),
        in_specs=[pl.BlockSpec((1, tm, D_MODEL), row)] + _halo_specs(tm, S, D_MODEL) + [
            pl.BlockSpec((1, 1, 6 * D_MODEL), lambda b, i: (b, 0, 0)),
            _const_spec((1, D_MODEL)),
            _const_spec((D_MODEL, 2 * D_FF)),
            _const_spec((3, 2 * D_FF)),
            _const_spec((1, 2 * D_FF)),
            _const_spec((D_FF, D_MODEL)),
            _const_spec((1, D_MODEL)),
        ],
        out_specs=pl.BlockSpec((1, tm, D_MODEL), row),
        compiler_params=_params(("parallel", "parallel")),
        name="conv_ffn",
    )(x, x, x, mod, prep["norm2_g"], prep["w_up"], prep["conv_f_w"], prep["conv_f_b"],
      prep["w_down"], prep["final_g"])


def _head_groups(w, n_heads, width, pieces):
    rows = w.shape[0]
    wh = w.reshape(rows, n_heads, width)
    out = jnp.zeros((rows, n_heads, LANES), F32)
    for src, n, dst, sign in pieces:
        out = out.at[:, :, dst:dst + n].set(sign * wh[:, :, src:src + n])
    return out.reshape(rows, n_heads * LANES)


def _prepare(S, norm1_g, w_in, b_gate, q_norm_g, kv_norm_g, w_uq, w_ukv, conv_m_w, conv_m_b,
             mh_norm_g, w_out, norm2_g, w_up, conv_f_w, conv_f_b, w_down, final_g):
    half = D_ROPE // 2
    r1 = D_NOPE
    r2 = D_NOPE + half
    c1, c2 = Q_RANK + KV_RANK, Q_RANK + KV_RANK + D_ROPE
    c3 = c2 + 2 * D_MLSTM
    c4 = c3 + D_MLSTM
    c5 = c4 + D_MLSTM

    kr = w_in[:, c1:c2]
    zeros = lambda n: jnp.zeros((D_MODEL, n), F32)
    kr_main = jnp.concatenate([zeros(r1), kr[:, :half], kr[:, half:], zeros(LANES - r2 - half)], axis=1)
    kr_swap = jnp.concatenate([zeros(r1), -kr[:, half:], kr[:, :half], zeros(LANES - r2 - half)], axis=1)
    gates = jnp.concatenate([w_in[:, c5:], zeros(LANES - 4 * M_HEADS)], axis=1)
    w_prep = jnp.concatenate([w_in[:, :c1], kr_main, kr_swap, w_in[:, c2:c3], w_in[:, c4:c5], gates], axis=1)

    qs = (D_NOPE + D_ROPE) ** -0.5 * math.log2(math.e)
    dq = D_NOPE + D_ROPE
    wq_main = _head_groups(w_uq, N_HEADS, dq, [(0, D_NOPE, 0, 1.0), (D_NOPE, half, r1, 1.0),
                                                (D_NOPE + half, half, r2, 1.0)])
    wq_swap = _head_groups(w_uq, N_HEADS, dq, [(D_NOPE + half, half, r1, -1.0), (D_NOPE, half, r2, 1.0)])
    w_q = jnp.concatenate([wq_main, wq_swap], axis=1) * qs

    dkv = D_NOPE + D_V
    wk = _head_groups(w_ukv, N_HEADS, dkv, [(0, D_NOPE, 0, 1.0)])
    wv = _head_groups(w_ukv, N_HEADS, dkv, [(D_NOPE, D_V, 0, 1.0)])

    inv = 1.0 / (ROPE_THETA ** (jnp.arange(half, dtype=F32) * (2.0 / D_ROPE)))
    ang = jnp.arange(S, dtype=jnp.int32).astype(F32)[:, None] * inv[None, :]
    ones = jnp.ones((S, r1), F32)
    pad = jnp.zeros((S, LANES - r2 - half), F32)
    cos = jnp.concatenate([ones, jnp.cos(ang), jnp.cos(ang), pad], axis=1)
    sin = jnp.concatenate([0.0 * ones, jnp.sin(ang), jnp.sin(ang), pad], axis=1)

    v_one = jnp.zeros((N_HEADS, LANES), F32).at[:, D_V].set(1.0).reshape(N_HEADS * LANES, 1)
    b_g = jnp.concatenate([b_gate, jnp.zeros((LANES - 4 * M_HEADS,), F32)]).reshape(1, LANES)

    d_attn = N_HEADS * D_V
    w_oa = jnp.zeros((N_HEADS, LANES, D_MODEL), F32).at[:, :D_V, :].set(
        w_out[:d_attn].reshape(N_HEADS, D_V, D_MODEL)).reshape(N_HEADS * LANES, D_MODEL)

    return dict(
        norm1_g=norm1_g.reshape(1, -1), w_in=w_prep.astype(BF16), w_vm_t=w_in[:, c3:c4].T.astype(BF16), b_gate=b_g,
        q_norm_g=q_norm_g.reshape(1, -1), kv_norm_g=kv_norm_g.reshape(1, -1),
        w_q=w_q.astype(BF16), w_k=wk.astype(BF16), w_vt=wv.T.astype(BF16), cos=cos, sin=sin, v_one=v_one,
        conv_m_w=conv_m_w, conv_m_b=conv_m_b.reshape(1, -1), mh_norm_g=mh_norm_g.reshape(1, -1),
        w_out_attn=w_oa.astype(BF16), w_out_mem=w_out[d_attn:].astype(BF16),
        norm2_g=norm2_g.reshape(1, -1), w_up=w_up.astype(BF16), conv_f_w=conv_f_w,
        conv_f_b=conv_f_b.reshape(1, -1), w_down=w_down.astype(BF16), final_g=final_g.reshape(1, -1),
    )


def _tiles(S):
    tm = min(512, S // 2)
    sub = 2 if S % (4 * tm) == 0 else 1
    return tm, min(512, S), sub


def _trunk(x, c, w_ada, b_ada, prep):
    S = x.shape[1]
    tm, tq, sub = _tiles(S)
    mod = _ada_mod(c, w_ada, b_ada)
    q, k, vt, qk_act, v_m, o_m, gates = _in_proj(x, mod, prep, tm)
    attn = _attention(q, k, vt, tq, sub)
    h_f, h_b = _mlstm(qk_act, v_m, gates)
    x1 = _out_proj(x, mod, attn, h_f, h_b, o_m, prep, tm)
    return _conv_ffn(x1, mod, prep, tm)


def kernel(x_prompt, x_sample, c_prompt, c_sample, norm1_g, w_ada, b_ada, w_in, b_gate, q_norm_g, kv_norm_g, w_uq, w_ukv, conv_m_w, conv_m_b, mh_norm_g, w_out, norm2_g, w_up, conv_f_w, conv_f_b, w_down, final_g):
    assert w_ada.shape[0] == 1, "single-layer trunk"
    assert x_prompt.shape[1] == x_sample.shape[1]
    prep = _prepare(x_prompt.shape[1], norm1_g[0], w_in[0], b_gate[0], q_norm_g[0], kv_norm_g[0], w_uq[0],
                    w_ukv[0], conv_m_w[0], conv_m_b[0], mh_norm_g[0], w_out[0], norm2_g[0], w_up[0],
                    conv_f_w[0], conv_f_b[0], w_down[0], final_g)
    y_prompt = _trunk(x_prompt, c_prompt, w_ada[0], b_ada[0], prep)
    y_sample = _trunk(x_sample, c_sample, w_ada[0], b_ada[0], prep)
    return (y_prompt, y_sample)
```

```python
import functools
import math

import jax
import jax.numpy as jnp
from jax import lax
from jax.experimental import pallas as pl
from jax.experimental.pallas import tpu as pltpu

F32 = jnp.float32
BF16 = jnp.bfloat16

D_MODEL = 1024
N_HEADS = 8
D_NOPE = 64
D_ROPE = 32
D_V = 64
Q_RANK = 384
KV_RANK = 256
M_HEADS = 4
M_DIM = 128
D_MLSTM = M_HEADS * M_DIM
CHUNK = 128
M_CPS = 4
D_FF = 2816
ROPE_THETA = 10000.0
EPS = 1e-6
LANES = 128
SUBLANES = 8
FF_CHUNK = 256
FFN_ROWS = 256
ST_ROWS = M_DIM + 16
ATT_HPS = 2
PV_ROWS = 80
VMEM_LIMIT = 56 * 1024 * 1024

_A_W = Q_RANK + KV_RANK + 2 * LANES
_QK0 = _A_W
_O0 = _QK0 + 2 * D_MLSTM
_G0 = _O0 + D_MLSTM
_N_PREP = _G0 + LANES


def _dot(a, b):
    return jnp.dot(a, b, preferred_element_type=F32)


def _dot_nt(a, b):
    return lax.dot_general(a, b, (((1,), (1,)), ((), ())), preferred_element_type=F32)


def _dot_tn(a, b):
    return lax.dot_general(a, b, (((0,), (0,)), ((), ())), preferred_element_type=F32)


def _rmsnorm(x, g):
    return x * lax.rsqrt(jnp.mean(x * x, axis=-1, keepdims=True) + EPS) * g


def _log_sigmoid(x):
    return jnp.minimum(x, 0.0) - jnp.log1p(jnp.exp(-jnp.abs(x)))


def _params(sem):
    return pltpu.CompilerParams(dimension_semantics=sem, vmem_limit_bytes=VMEM_LIMIT)


def _const_spec(shape):
    nd = len(shape)
    return pl.BlockSpec(shape, lambda *_: (0,) * nd, pipeline_mode=pl.Buffered(1))


def _halo_specs(tm, S, width):
    per = tm // SUBLANES
    nsub = S // SUBLANES
    return [
        pl.BlockSpec((1, SUBLANES, width), lambda b, i: (b, jnp.maximum(i * per - 1, 0), 0)),
        pl.BlockSpec((1, SUBLANES, width), lambda b, i: (b, jnp.minimum((i + 1) * per, nsub - 1), 0)),
    ]


def _conv3(u, cw_ref, cb_ref, cols, tm):
    rows = tm + 2 * SUBLANES
    full = (cw_ref[0:1, cols] * pltpu.roll(u, 1, axis=0) + cw_ref[1:2, cols] * u
            + cw_ref[2:3, cols] * pltpu.roll(u, rows - 1, axis=0) + cb_ref[:, cols])
    return full[SUBLANES:SUBLANES + tm]


def _ada_kernel(c_ref, w_ref, b_ref, o_ref):
    c = c_ref[...]
    a = (c * jax.nn.sigmoid(c)).astype(BF16)
    o_ref[...] = _dot(a, w_ref[...].astype(BF16)) + b_ref[...]


def _ada_mod(c, w_ada, b_ada):
    B = c.shape[0]
    cp = jnp.zeros((SUBLANES, D_MODEL), F32).at[:B].set(c)
    n = 6 * D_MODEL
    out = pl.pallas_call(
        _ada_kernel,
        out_shape=jax.ShapeDtypeStruct((SUBLANES, n), F32),
        grid=(6,),
        in_specs=[
            pl.BlockSpec((SUBLANES, D_MODEL), lambda j: (0, 0)),
            pl.BlockSpec((D_MODEL, D_MODEL), lambda j: (0, j)),
            pl.BlockSpec((1, D_MODEL), lambda j: (0, j)),
        ],
        out_specs=pl.BlockSpec((SUBLANES, D_MODEL), lambda j: (0, j)),
        compiler_params=_params(("parallel",)),
        name="ada_mod",
    )(cp, w_ada, b_ada.reshape(1, n))
    return out[:B].reshape(B, 1, n)


def _in_proj_kernel(x_ref, xp_ref, xn_ref, mod_ref, n1_ref, w_ref, bg_ref, qn_ref, kvn_ref, wq_ref, wk_ref,
                    wvt_ref, cos_ref, sin_ref, one_ref, cw_ref, cb_ref, wvm_ref,
                    q_ref, k_ref, vt_ref, qk_ref, vm_ref, om_ref, g_ref):
    i = pl.program_id(1)
    nt = pl.num_programs(1)
    tm = x_ref.shape[1]
    sh1 = mod_ref[0, :, 0:D_MODEL]
    sc1 = mod_ref[0, :, D_MODEL:2 * D_MODEL]

    def norm_mod(r):
        return _rmsnorm(r, n1_ref[...]) * (1.0 + sc1) + sh1

    hf = norm_mod(x_ref[0])
    h = hf.astype(BF16)
    hp = jnp.where(i > 0, norm_mod(xp_ref[0]), 0.0)
    hn = jnp.where(i < nt - 1, norm_mod(xn_ref[0]), 0.0)
    hext = jnp.concatenate([hp, hf, hn], axis=0).astype(BF16)
    conv = _conv3(_dot(hext, w_ref[:, _QK0:_O0]), cw_ref, cb_ref, slice(None), tm)
    qk = conv * jax.nn.sigmoid(conv)
    qk_ref[0, :, 0:D_MLSTM] = (qk[:, 0:D_MLSTM] * (M_DIM ** -0.5)).astype(BF16)
    qk_ref[0, :, D_MLSTM:] = qk[:, D_MLSTM:].astype(BF16)

    pa = _dot(h, w_ref[:, 0:_A_W])
    vm_ref[0] = _dot_nt(wvm_ref[...], h).astype(BF16)
    om_ref[0] = _dot(h, w_ref[:, _O0:_G0])
    g = _dot(h, w_ref[:, _G0:_N_PREP]) + bg_ref[...]
    lane = lax.broadcasted_iota(jnp.int32, g.shape, 1)
    is_forget = (lane < 4 * M_HEADS) & ((lane // M_HEADS) % 2 == 1)
    g_ref[0] = jnp.where(is_forget, _log_sigmoid(g), g)

    cos = cos_ref[...]
    sin = sin_ref[...]
    cos8 = jnp.concatenate([cos] * N_HEADS, axis=1)
    sin8 = jnp.concatenate([sin] * N_HEADS, axis=1)
    hw = N_HEADS * LANES

    cqn = _rmsnorm(pa[:, 0:Q_RANK], qn_ref[...]).astype(BF16)
    qq = _dot(cqn, wq_ref[...])
    q = (qq[:, 0:hw] * cos8 + qq[:, hw:2 * hw] * sin8).astype(BF16)

    ckvn = _rmsnorm(pa[:, Q_RANK:Q_RANK + KV_RANK], kvn_ref[...]).astype(BF16)
    kk = _dot(ckvn, wk_ref[...])
    kr0 = Q_RANK + KV_RANK
    kr = pa[:, kr0:kr0 + LANES] * cos + pa[:, kr0 + LANES:kr0 + 2 * LANES] * sin
    k = (kk + jnp.concatenate([kr] * N_HEADS, axis=1)).astype(BF16)
    vt = (_dot_nt(wvt_ref[...], ckvn) + one_ref[...]).astype(BF16)
    for hd in range(N_HEADS):
        sl = slice(hd * LANES, (hd + 1) * LANES)
        q_ref[0, hd] = q[:, sl]
        k_ref[0, hd] = k[:, sl]
        vt_ref[0, hd, 0] = vt[sl, :]


def _in_proj(x, mod, prep, tm):
    B, S, _ = x.shape
    grid = (B, S // tm)
    row = lambda b, i: (b, i, 0)
    hrow = lambda b, i: (b, 0, i, 0)
    head_shape = jax.ShapeDtypeStruct((B, N_HEADS, S, LANES), BF16)
    hw = N_HEADS * LANES
    return pl.pallas_call(
        _in_proj_kernel,
        out_shape=(
            head_shape, head_shape,
            jax.ShapeDtypeStruct((B, N_HEADS, S // tm, LANES, tm), BF16),
            jax.ShapeDtypeStruct((B, S, 2 * D_MLSTM), BF16),
            jax.ShapeDtypeStruct((B, D_MLSTM, S), BF16),
            jax.ShapeDtypeStruct((B, S, D_MLSTM), F32),
            jax.ShapeDtypeStruct((B, S, LANES), F32),
        ),
        grid=grid,
        in_specs=[pl.BlockSpec((1, tm, D_MODEL), row)] + _halo_specs(tm, S, D_MODEL) + [
            pl.BlockSpec((1, 1, 6 * D_MODEL), lambda b, i: (b, 0, 0)),
            _const_spec((1, D_MODEL)),
            _const_spec((D_MODEL, _N_PREP)),
            _const_spec((1, LANES)),
            _const_spec((1, Q_RANK)),
            _const_spec((1, KV_RANK)),
            _const_spec((Q_RANK, 2 * hw)),
            _const_spec((KV_RANK, hw)),
            _const_spec((hw, KV_RANK)),
            pl.BlockSpec((tm, LANES), lambda b, i: (i, 0)),
            pl.BlockSpec((tm, LANES), lambda b, i: (i, 0)),
            _const_spec((hw, 1)),
            _const_spec((3, 2 * D_MLSTM)),
            _const_spec((1, 2 * D_MLSTM)),
            _const_spec((D_MLSTM, D_MODEL)),
        ],
        out_specs=(
            pl.BlockSpec((1, N_HEADS, tm, LANES), hrow),
            pl.BlockSpec((1, N_HEADS, tm, LANES), hrow),
            pl.BlockSpec((1, N_HEADS, 1, LANES, tm), lambda b, i: (b, 0, i, 0, 0)),
            pl.BlockSpec((1, tm, 2 * D_MLSTM), row),
            pl.BlockSpec((1, D_MLSTM, tm), lambda b, i: (b, 0, i)),
            pl.BlockSpec((1, tm, D_MLSTM), row),
            pl.BlockSpec((1, tm, LANES), row),
        ),
        compiler_params=_params(("parallel", "parallel")),
        name="in_proj",
    )(x, x, x, mod, prep["norm1_g"], prep["w_in"], prep["b_gate"], prep["q_norm_g"], prep["kv_norm_g"],
      prep["w_q"], prep["w_k"], prep["w_vt"], prep["cos"], prep["sin"], prep["v_one"],
      prep["conv_m_w"], prep["conv_m_b"], prep["w_vm_t"])


def _attn_kernel(q_ref, qn_ref, k_ref, vt_ref, o_ref, s_scr, mx_scr, *, sub):
    i = pl.program_id(2)
    hps = q_ref.shape[1]
    ck = vt_ref.shape[4]
    tk = sub * ck
    n = k_ref.shape[2] // tk

    def scores(hh, qt, c, slot):
        s = _dot(k_ref[0, hh, c * tk:(c + 1) * tk, :], qt)
        s_scr[hh, slot] = s
        return jnp.max(s, axis=0, keepdims=True)

    qts = [q_ref[0, hh].astype(F32).T.astype(BF16) for hh in range(hps)]

    @pl.when(i == 0)
    def _():
        for hh in range(hps):
            mx_scr[hh] = scores(hh, qts[hh], 0, 0)

    mx = [mx_scr[hh] for hh in range(hps)]
    m = [None] * hps
    acc = [None] * hps
    for c in range(n):
        for hh in range(hps):
            if c + 1 < n:
                mx_next = scores(hh, qts[hh], c + 1, (c + 1) % 2)
            else:
                mx_next = scores(hh, qn_ref[0, hh].astype(F32).T.astype(BF16), 0, 0)
            m_new = mx[hh] if c == 0 else jnp.maximum(m[hh], mx[hh])
            p = jnp.exp2(s_scr[hh, c % 2] - m_new).astype(BF16)
            pv = _dot(vt_ref[0, hh, c * sub, 0:PV_ROWS, :], p[0:ck])
            for u in range(1, sub):
                pv = pv + _dot(vt_ref[0, hh, c * sub + u, 0:PV_ROWS, :], p[u * ck:(u + 1) * ck])
            acc[hh] = pv if c == 0 else jnp.exp2(m[hh] - m_new) * acc[hh] + pv
            m[hh], mx[hh] = m_new, mx_next
    for hh in range(hps):
        mx_scr[hh] = mx[hh]
        out = acc[hh][0:D_V] / acc[hh][D_V:D_V + 1]
        o_ref[0, hh] = jnp.concatenate([out, jnp.zeros((LANES - D_V, out.shape[1]), F32)],
                                       axis=0).T.astype(BF16)


def _attention(q, k, vt, tq, sub):
    B, H, S, _ = q.shape
    ck = vt.shape[4]
    tk = sub * ck
    nq = S // tq
    assert S % (2 * tk) == 0 and S % tq == 0
    return pl.pallas_call(
        functools.partial(_attn_kernel, sub=sub),
        out_shape=jax.ShapeDtypeStruct((B, H, S, LANES), BF16),
        grid=(B, H // ATT_HPS, nq),
        in_specs=[
            pl.BlockSpec((1, ATT_HPS, tq, LANES), lambda b, h, i: (b, h, i, 0)),
            pl.BlockSpec((1, ATT_HPS, tq, LANES), lambda b, h, i: (b, h, jnp.minimum(i + 1, nq - 1), 0)),
            pl.BlockSpec((1, ATT_HPS, S, LANES), lambda b, h, i: (b, h, 0, 0), pipeline_mode=pl.Buffered(1)),
            pl.BlockSpec((1, ATT_HPS, S // ck, LANES, ck), lambda b, h, i: (b, h, 0, 0, 0),
                         pipeline_mode=pl.Buffered(1)),
        ],
        out_specs=pl.BlockSpec((1, ATT_HPS, tq, LANES), lambda b, h, i: (b, h, i, 0)),
        scratch_shapes=[pltpu.VMEM((ATT_HPS, 2, tk, tq), F32), pltpu.VMEM((ATT_HPS, 1, tq), F32)],
        compiler_params=_params(("parallel", "parallel", "arbitrary")),
        name="attention",
    )(q, q, k, vt)


def _split3(x):
    hi = x.astype(BF16)
    r1 = x - hi.astype(F32)
    mid = r1.astype(BF16)
    lo = (r1 - mid.astype(F32)).astype(BF16)
    return hi, mid, lo


def _mlstm_direction(qk_ref, vt_ref, g_ref, h_ref, c_scr, m_scr, *, sub, reverse):
    L = CHUNK
    rs = slice(sub * L, (sub + 1) * L)
    g = g_ref[0, rs, :]
    gt = g.T
    si = lax.broadcasted_iota(jnp.int32, (L, L), 0)
    ti = lax.broadcasted_iota(jnp.int32, (L, L), 1)
    vis = (si >= ti) if reverse else (si <= ti)
    vis_bf = jnp.where(vis, 1.0, 0.0).astype(BF16)
    vis_t_bf = jnp.where((ti >= si) if reverse else (ti <= si), 1.0, 0.0).astype(BF16)
    b_cols = sum(_dot(vis_t_bf, piece) for piece in _split3(g))
    b_rows = sum(_dot(piece, vis_bf) for piece in _split3(gt))
    one_rows = jnp.where(lax.broadcasted_iota(jnp.int32, (ST_ROWS - M_DIM, L), 0) == 0, 1.0, 0.0).astype(BF16)

    i_off = 2 * M_HEADS if reverse else 0
    f_off = i_off + M_HEADS
    last = 0 if reverse else L - 1
    for hd in range(M_HEADS):
        sl = slice(hd * M_DIM, (hd + 1) * M_DIM)
        qb = qk_ref[0, rs, sl]
        kb = qk_ref[0, rs, D_MLSTM + hd * M_DIM:D_MLSTM + (hd + 1) * M_DIM]
        vt_ext = jnp.concatenate([vt_ref[0, sl, rs], one_rows], axis=0)
        b_r = b_rows[f_off + hd:f_off + hd + 1, :]
        li_r = gt[i_off + hd:i_off + hd + 1, :]
        u_c = g[:, i_off + hd:i_off + hd + 1] - b_cols[:, f_off + hd:f_off + hd + 1]
        m_prev = m_scr[hd]
        c_prev = c_scr[hd]

        dmat = jnp.where(vis, b_r + u_c, -jnp.inf)
        inter = b_r + m_prev
        m_t = jnp.maximum(inter, jnp.max(dmat, axis=0, keepdims=True))
        w_intra = jnp.exp(dmat - m_t)
        w_state = jnp.exp(inter - m_t)
        s = _dot_nt(kb, qb) * w_intra
        tot = _dot(vt_ext, s.astype(BF16)) + w_state * _dot_nt(c_prev.astype(BF16), qb)
        den = tot[M_DIM:M_DIM + 1, :]
        h_t = tot[0:M_DIM] / jnp.maximum(jnp.abs(den), jnp.exp(-m_t))
        h_ref[0, rs, sl] = h_t.T

        b_last = b_r[:, last:last + 1]
        a = b_last - b_r + li_r
        m_new = jnp.maximum(b_last + m_prev, jnp.max(a, axis=1, keepdims=True))
        w_s = jnp.exp(a - m_new)
        decay = jnp.exp(b_last + m_prev - m_new)
        vw = (vt_ext.astype(F32) * w_s).astype(BF16)
        c_scr[hd] = decay * c_prev + _dot(vw, kb)
        m_scr[hd] = m_new


def _mlstm_kernel(qkf_ref, vf_ref, gf_ref, qkb_ref, vb_ref, gb_ref,
                  hf_ref, hb_ref, cf_scr, mf_scr, cb_scr, mb_scr):
    @pl.when(pl.program_id(1) == 0)
    def _():
        for r in (cf_scr, mf_scr, cb_scr, mb_scr):
            r[...] = jnp.zeros_like(r)

    for j in range(M_CPS):
        _mlstm_direction(qkf_ref, vf_ref, gf_ref, hf_ref, cf_scr, mf_scr, sub=j, reverse=False)
        _mlstm_direction(qkb_ref, vb_ref, gb_ref, hb_ref, cb_scr, mb_scr, sub=M_CPS - 1 - j, reverse=True)


def _mlstm(qk_act, v_m, gates):
    B, S, _ = qk_act.shape
    rows = M_CPS * CHUNK
    nc = S // rows

    def specs(cidx):
        main = lambda b, i: (b, cidx(i), 0)
        return [
            pl.BlockSpec((1, rows, 2 * D_MLSTM), main),
            pl.BlockSpec((1, D_MLSTM, rows), lambda b, i: (b, 0, cidx(i))),
            pl.BlockSpec((1, rows, LANES), main),
        ]

    fwd = lambda i: i
    bwd = lambda i: nc - 1 - i
    state = [pltpu.VMEM((M_HEADS, ST_ROWS, M_DIM), F32), pltpu.VMEM((M_HEADS, 1, 1), F32)]
    h_shape = jax.ShapeDtypeStruct((B, S, D_MLSTM), F32)
    return pl.pallas_call(
        _mlstm_kernel,
        out_shape=(h_shape, h_shape),
        grid=(B, nc),
        in_specs=specs(fwd) + specs(bwd),
        out_specs=(
            pl.BlockSpec((1, rows, D_MLSTM), lambda b, i: (b, i, 0)),
            pl.BlockSpec((1, rows, D_MLSTM), lambda b, i: (b, nc - 1 - i, 0)),
        ),
        scratch_shapes=state + state,
        compiler_params=_params(("parallel", "arbitrary")),
        name="mlstm",
    )(qk_act, v_m, gates, qk_act, v_m, gates)


def _out_proj_kernel(x_ref, mod_ref, a_ref, hf_ref, hb_ref, om_ref, mg_ref, wa_ref, wm_ref, o_ref):
    g1 = mod_ref[0, :, 2 * D_MODEL:3 * D_MODEL]
    attn = jnp.concatenate([a_ref[0, hd] for hd in range(N_HEADS)], axis=1)
    y = _dot(attn, wa_ref[...])
    h = hf_ref[0] + hb_ref[0]
    om = om_ref[0]
    parts = []
    for hd in range(M_HEADS):
        sl = slice(hd * M_DIM, (hd + 1) * M_DIM)
        hh = h[:, sl]
        mu = jnp.mean(hh, axis=1, keepdims=True)
        d = hh - mu
        var = jnp.mean(d * d, axis=1, keepdims=True)
        parts.append(d * lax.rsqrt(var + EPS) * mg_ref[:, sl] * jax.nn.sigmoid(om[:, sl]))
    mem = jnp.concatenate(parts, axis=1).astype(BF16)
    y = y + _dot(mem, wm_ref[...])
    o_ref[0] = x_ref[0] + g1 * y


def _out_proj(x, mod, attn, h_f, h_b, o_m, prep, tm):
    B, S, _ = x.shape
    row = lambda b, i: (b, i, 0)
    return pl.pallas_call(
        _out_proj_kernel,
        out_shape=jax.ShapeDtypeStruct((B, S, D_MODEL), F32),
        grid=(B, S // tm),
        in_specs=[
            pl.BlockSpec((1, tm, D_MODEL), row),
            pl.BlockSpec((1, 1, 6 * D_MODEL), lambda b, i: (b, 0, 0)),
            pl.BlockSpec((1, N_HEADS, tm, LANES), lambda b, i: (b, 0, i, 0)),
            pl.BlockSpec((1, tm, D_MLSTM), row),
            pl.BlockSpec((1, tm, D_MLSTM), row),
            pl.BlockSpec((1, tm, D_MLSTM), row),
            _const_spec((1, D_MLSTM)),
            _const_spec((N_HEADS * LANES, D_MODEL)),
            _const_spec((D_MLSTM, D_MODEL)),
        ],
        out_specs=pl.BlockSpec((1, tm, D_MODEL), row),
        compiler_params=_params(("parallel", "parallel")),
        name="out_proj",
    )(x, mod, attn, h_f, h_b, o_m, prep["mh_norm_g"], prep["w_out_attn"], prep["w_out_mem"])


def _ffn_kernel(x_ref, xp_ref, xn_ref, mod_ref, n2_ref, wu_ref, cw_ref, cb_ref, wd_ref, fg_ref, o_ref):
    i = pl.program_id(1)
    nt = pl.num_programs(1)
    tm = x_ref.shape[1]
    sh2 = mod_ref[0, :, 3 * D_MODEL:4 * D_MODEL]
    sc2 = mod_ref[0, :, 4 * D_MODEL:5 * D_MODEL]
    g2 = mod_ref[0, :, 5 * D_MODEL:6 * D_MODEL]

    def norm_mod(r):
        return _rmsnorm(r, n2_ref[...]) * (1.0 + sc2) + sh2

    x = x_ref[0]
    hp = jnp.where(i > 0, norm_mod(xp_ref[0]), 0.0)
    hn = jnp.where(i < nt - 1, norm_mod(xn_ref[0]), 0.0)
    hext = jnp.concatenate([hp, norm_mod(x), hn], axis=0).astype(BF16)

    n_chunks = D_FF // FF_CHUNK
    cols = [slice(2 * j * FF_CHUNK, 2 * (j + 1) * FF_CHUNK) for j in range(n_chunks)]
    acc = jnp.zeros((tm, D_MODEL), F32)
    u_next = _dot(hext, wu_ref[:, cols[0]])
    for j in range(n_chunks):
        u_raw = u_next
        if j + 1 < n_chunks:
            u_next = _dot(hext, wu_ref[:, cols[j + 1]])
        u = _conv3(u_raw, cw_ref, cb_ref, cols[j], tm)
        a = u[:, 0:FF_CHUNK]
        g = u[:, FF_CHUNK:]
        act = (g * jax.nn.sigmoid(g) * a).astype(BF16)
        acc = acc + _dot(act, wd_ref[j * FF_CHUNK:(j + 1) * FF_CHUNK, :])
    o_ref[0] = _rmsnorm(x + g2 * acc, fg_ref[...])


def _conv_ffn(x, mod, prep, tm):
    B, S, _ = x.shape
    row = lambda b, i: (b, i, 0)
    return pl.pallas_call(
        _ffn_kernel,
        out_shape=jax.ShapeDtypeStruct((B, S, D_MODEL), F32),
        grid=(B, S // tm),
        in_specs=[pl.BlockSpec((1, tm, D_MODEL), row)] + _halo_specs(tm, S, D_MODEL) + [
            pl.BlockSpec((1, 1, 6 * D_MODEL), lambda b, i: (b, 0, 0)),
            _const_spec((1, D_MODEL)),
            _const_spec((D_MODEL, 2 * D_FF)),
            _const_spec((3, 2 * D_FF)),
            _const_spec((1, 2 * D_FF)),
            _const_spec((D_FF, D_MODEL)),
            _const_spec((1, D_MODEL)),
        ],
        out_specs=pl.BlockSpec((1, tm, D_MODEL), row),
        compiler_params=_params(("parallel", "parallel")),
        name="conv_ffn",
    )(x, x, x, mod, prep["norm2_g"], prep["w_up"], prep["conv_f_w"], prep["conv_f_b"],
      prep["w_down"], prep["final_g"])


def _head_groups(w, n_heads, width, pieces):
    rows = w.shape[0]
    wh = w.reshape(rows, n_heads, width)
    out = jnp.zeros((rows, n_heads, LANES), F32)
    for src, n, dst, sign in pieces:
        out = out.at[:, :, dst:dst + n].set(sign * wh[:, :, src:src + n])
    return out.reshape(rows, n_heads * LANES)


def _interleave_ff(w):
    rows = w.shape[0]
    return w.reshape(rows, 2, D_FF // FF_CHUNK, FF_CHUNK).transpose(0, 2, 1, 3).reshape(rows, 2 * D_FF)


def _prepare(S, norm1_g, w_in, b_gate, q_norm_g, kv_norm_g, w_uq, w_ukv, conv_m_w, conv_m_b,
             mh_norm_g, w_out, norm2_g, w_up, conv_f_w, conv_f_b, w_down, final_g):
    half = D_ROPE // 2
    r1 = D_NOPE
    r2 = D_NOPE + half
    c1, c2 = Q_RANK + KV_RANK, Q_RANK + KV_RANK + D_ROPE
    c3 = c2 + 2 * D_MLSTM
    c4 = c3 + D_MLSTM
    c5 = c4 + D_MLSTM

    kr = w_in[:, c1:c2]
    zeros = lambda n: jnp.zeros((D_MODEL, n), F32)
    kr_main = jnp.concatenate([zeros(r1), kr[:, :half], kr[:, half:], zeros(LANES - r2 - half)], axis=1)
    kr_swap = jnp.concatenate([zeros(r1), -kr[:, half:], kr[:, :half], zeros(LANES - r2 - half)], axis=1)
    gates = jnp.concatenate([w_in[:, c5:], zeros(LANES - 4 * M_HEADS)], axis=1)
    w_prep = jnp.concatenate([w_in[:, :c1], kr_main, kr_swap, w_in[:, c2:c3], w_in[:, c4:c5], gates], axis=1)

    qs = (D_NOPE + D_ROPE) ** -0.5 * math.log2(math.e)
    dq = D_NOPE + D_ROPE
    wq_main = _head_groups(w_uq, N_HEADS, dq, [(0, D_NOPE, 0, 1.0), (D_NOPE, half, r1, 1.0),
                                                (D_NOPE + half, half, r2, 1.0)])
    wq_swap = _head_groups(w_uq, N_HEADS, dq, [(D_NOPE + half, half, r1, -1.0), (D_NOPE, half, r2, 1.0)])
    w_q = jnp.concatenate([wq_main, wq_swap], axis=1) * qs

    dkv = D_NOPE + D_V
    wk = _head_groups(w_ukv, N_HEADS, dkv, [(0, D_NOPE, 0, 1.0)])
    wv = _head_groups(w_ukv, N_HEADS, dkv, [(D_NOPE, D_V, 0, 1.0)])

    inv = 1.0 / (ROPE_THETA ** (jnp.arange(half, dtype=F32) * (2.0 / D_ROPE)))
    ang = jnp.arange(S, dtype=jnp.int32).astype(F32)[:, None] * inv[None, :]
    ones = jnp.ones((S, r1), F32)
    pad = jnp.zeros((S, LANES - r2 - half), F32)
    cos = jnp.concatenate([ones, jnp.cos(ang), jnp.cos(ang), pad], axis=1)
    sin = jnp.concatenate([0.0 * ones, jnp.sin(ang), jnp.sin(ang), pad], axis=1)

    v_one = jnp.zeros((N_HEADS, LANES), F32).at[:, D_V].set(1.0).reshape(N_HEADS * LANES, 1)
    b_g = jnp.concatenate([b_gate, jnp.zeros((LANES - 4 * M_HEADS,), F32)]).reshape(1, LANES)

    d_attn = N_HEADS * D_V
    w_oa = jnp.zeros((N_HEADS, LANES, D_MODEL), F32).at[:, :D_V, :].set(
        w_out[:d_attn].reshape(N_HEADS, D_V, D_MODEL)).reshape(N_HEADS * LANES, D_MODEL)

    return dict(
        norm1_g=norm1_g.reshape(1, -1), w_in=w_prep.astype(BF16), w_vm_t=w_in[:, c3:c4].T.astype(BF16), b_gate=b_g,
        q_norm_g=q_norm_g.reshape(1, -1), kv_norm_g=kv_norm_g.reshape(1, -1),
        w_q=w_q.astype(BF16), w_k=wk.astype(BF16), w_vt=wv.T.astype(BF16), cos=cos, sin=sin, v_one=v_one,
        conv_m_w=conv_m_w, conv_m_b=conv_m_b.reshape(1, -1), mh_norm_g=mh_norm_g.reshape(1, -1),
        w_out_attn=w_oa.astype(BF16), w_out_mem=w_out[d_attn:].astype(BF16),
        norm2_g=norm2_g.reshape(1, -1), w_up=_interleave_ff(w_up).astype(BF16), conv_f_w=_interleave_ff(conv_f_w),
        conv_f_b=_interleave_ff(conv_f_b.reshape(1, -1)), w_down=w_down.astype(BF16),
        final_g=final_g.reshape(1, -1),
    )


def _tiles(S):
    tm = min(512, S // 2)
    sub = 2 if S % (4 * tm) == 0 else 1
    return tm, min(512, S), sub


def _trunk(x, c, w_ada, b_ada, prep):
    S = x.shape[1]
    tm, tq, sub = _tiles(S)
    mod = _ada_mod(c, w_ada, b_ada)
    q, k, vt, qk_act, v_m, o_m, gates = _in_proj(x, mod, prep, tm)
    attn = _attention(q, k, vt, tq, sub)
    h_f, h_b = _mlstm(qk_act, v_m, gates)
    x1 = _out_proj(x, mod, attn, h_f, h_b, o_m, prep, tm)
    return _conv_ffn(x1, mod, prep, min(FFN_ROWS, tm))


def kernel(x_prompt, x_sample, c_prompt, c_sample, norm1_g, w_ada, b_ada, w_in, b_gate, q_norm_g, kv_norm_g, w_uq, w_ukv, conv_m_w, conv_m_b, mh_norm_g, w_out, norm2_g, w_up, conv_f_w, conv_f_b, w_down, final_g):
    assert w_ada.shape[0] == 1, "single-layer trunk"
    assert x_prompt.shape[1] == x_sample.shape[1]
    prep = _prepare(x_prompt.shape[1], norm1_g[0], w_in[0], b_gate[0], q_norm_g[0], kv_norm_g[0], w_uq[0],
                    w_ukv[0], conv_m_w[0], conv_m_b[0], mh_norm_g[0], w_out[0], norm2_g[0], w_up[0],
                    conv_f_w[0], conv_f_b[0], w_down[0], final_g)
    y_prompt = _trunk(x_prompt, c_prompt, w_ada[0], b_ada[0], prep)
    y_sample = _trunk(x_sample, c_sample, w_ada[0], b_ada[0], prep)
    return (y_prompt, y_sample)
```

```python
import functools
import math

import jax
import jax.numpy as jnp
from jax import lax
from jax.experimental import pallas as pl
from jax.experimental.pallas import tpu as pltpu

F32 = jnp.float32
BF16 = jnp.bfloat16

D_MODEL = 1024
N_HEADS = 8
D_NOPE = 64
D_ROPE = 32
D_V = 64
Q_RANK = 384
KV_RANK = 256
M_HEADS = 4
M_DIM = 128
D_MLSTM = M_HEADS * M_DIM
CHUNK = 128
M_CPS = 4
D_FF = 2816
ROPE_THETA = 10000.0
EPS = 1e-6
LANES = 128
SUBLANES = 8
FF_CHUNK = 256
FFN_ROWS = 256
ST_ROWS = M_DIM + 16
ATT_HPS = 2
PV_ROWS = 80
VMEM_LIMIT = 56 * 1024 * 1024

_A_W = Q_RANK + KV_RANK + 2 * LANES
_QK0 = _A_W
_O0 = _QK0 + 2 * D_MLSTM
_G0 = _O0 + D_MLSTM
_N_PREP = _G0 + LANES


def _dot(a, b):
    return jnp.dot(a, b, preferred_element_type=F32)


def _dot_nt(a, b):
    return lax.dot_general(a, b, (((1,), (1,)), ((), ())), preferred_element_type=F32)


def _dot_tn(a, b):
    return lax.dot_general(a, b, (((0,), (0,)), ((), ())), preferred_element_type=F32)


def _rmsnorm(x, g):
    return x * lax.rsqrt(jnp.mean(x * x, axis=-1, keepdims=True) + EPS) * g


def _log_sigmoid(x):
    return jnp.minimum(x, 0.0) - jnp.log1p(jnp.exp(-jnp.abs(x)))


def _params(sem):
    return pltpu.CompilerParams(dimension_semantics=sem, vmem_limit_bytes=VMEM_LIMIT)


def _const_spec(shape):
    nd = len(shape)
    return pl.BlockSpec(shape, lambda *_: (0,) * nd, pipeline_mode=pl.Buffered(1))


def _halo_specs(tm, S, width):
    per = tm // SUBLANES
    nsub = S // SUBLANES
    return [
        pl.BlockSpec((1, SUBLANES, width), lambda b, i: (b, jnp.maximum(i * per - 1, 0), 0)),
        pl.BlockSpec((1, SUBLANES, width), lambda b, i: (b, jnp.minimum((i + 1) * per, nsub - 1), 0)),
    ]


def _conv3(u, cw_ref, cb_ref, cols, tm):
    rows = tm + 2 * SUBLANES
    full = (cw_ref[0:1, cols] * pltpu.roll(u, 1, axis=0) + cw_ref[1:2, cols] * u
            + cw_ref[2:3, cols] * pltpu.roll(u, rows - 1, axis=0) + cb_ref[:, cols])
    return full[SUBLANES:SUBLANES + tm]


def _ada_kernel(c_ref, w_ref, b_ref, o_ref):
    c = c_ref[...]
    a = (c * jax.nn.sigmoid(c)).astype(BF16)
    o_ref[...] = _dot(a, w_ref[...].astype(BF16)) + b_ref[...]


def _ada_mod(c, w_ada, b_ada):
    B = c.shape[0]
    cp = jnp.zeros((SUBLANES, D_MODEL), F32).at[:B].set(c)
    n = 6 * D_MODEL
    out = pl.pallas_call(
        _ada_kernel,
        out_shape=jax.ShapeDtypeStruct((SUBLANES, n), F32),
        grid=(6,),
        in_specs=[
            pl.BlockSpec((SUBLANES, D_MODEL), lambda j: (0, 0)),
            pl.BlockSpec((D_MODEL, D_MODEL), lambda j: (0, j)),
            pl.BlockSpec((1, D_MODEL), lambda j: (0, j)),
        ],
        out_specs=pl.BlockSpec((SUBLANES, D_MODEL), lambda j: (0, j)),
        compiler_params=_params(("parallel",)),
        name="ada_mod",
    )(cp, w_ada, b_ada.reshape(1, n))
    return out[:B].reshape(B, 1, n)


def _in_proj_kernel(x_ref, xp_ref, xn_ref, mod_ref, n1_ref, w_ref, bg_ref, qn_ref, kvn_ref, wq_ref, wk_ref,
                    wvt_ref, cos_ref, sin_ref, one_ref, cw_ref, cb_ref, wvm_ref,
                    q_ref, k_ref, vt_ref, qk_ref, vm_ref, om_ref, g_ref):
    i = pl.program_id(1)
    nt = pl.num_programs(1)
    tm = x_ref.shape[1]
    sh1 = mod_ref[0, :, 0:D_MODEL]
    sc1 = mod_ref[0, :, D_MODEL:2 * D_MODEL]

    def norm_mod(r):
        return _rmsnorm(r, n1_ref[...]) * (1.0 + sc1) + sh1

    hf = norm_mod(x_ref[0])
    h = hf.astype(BF16)
    hp = jnp.where(i > 0, norm_mod(xp_ref[0]), 0.0)
    hn = jnp.where(i < nt - 1, norm_mod(xn_ref[0]), 0.0)
    hext = jnp.concatenate([hp, hf, hn], axis=0).astype(BF16)
    conv = _conv3(_dot(hext, w_ref[:, _QK0:_O0]), cw_ref, cb_ref, slice(None), tm)
    qk = conv * jax.nn.sigmoid(conv)
    qk_ref[0, :, 0:D_MLSTM] = (qk[:, 0:D_MLSTM] * (M_DIM ** -0.5)).astype(BF16)
    qk_ref[0, :, D_MLSTM:] = qk[:, D_MLSTM:].astype(BF16)

    pa = _dot(h, w_ref[:, 0:_A_W])
    vm_ref[0] = _dot_nt(wvm_ref[...], h).astype(BF16)
    om_ref[0] = _dot(h, w_ref[:, _O0:_G0])
    g = _dot(h, w_ref[:, _G0:_N_PREP]) + bg_ref[...]
    lane = lax.broadcasted_iota(jnp.int32, g.shape, 1)
    is_forget = (lane < 4 * M_HEADS) & ((lane // M_HEADS) % 2 == 1)
    g_ref[0] = jnp.where(is_forget, _log_sigmoid(g), g)

    cos = cos_ref[...]
    sin = sin_ref[...]
    cos8 = jnp.concatenate([cos] * N_HEADS, axis=1)
    sin8 = jnp.concatenate([sin] * N_HEADS, axis=1)
    hw = N_HEADS * LANES

    cqn = _rmsnorm(pa[:, 0:Q_RANK], qn_ref[...]).astype(BF16)
    qq = _dot(cqn, wq_ref[...])
    q = (qq[:, 0:hw] * cos8 + qq[:, hw:2 * hw] * sin8).astype(BF16)

    ckvn = _rmsnorm(pa[:, Q_RANK:Q_RANK + KV_RANK], kvn_ref[...]).astype(BF16)
    kk = _dot(ckvn, wk_ref[...])
    kr0 = Q_RANK + KV_RANK
    kr = pa[:, kr0:kr0 + LANES] * cos + pa[:, kr0 + LANES:kr0 + 2 * LANES] * sin
    k = (kk + jnp.concatenate([kr] * N_HEADS, axis=1)).astype(BF16)
    vt = (_dot_nt(wvt_ref[...], ckvn) + one_ref[...]).astype(BF16)
    for hd in range(N_HEADS):
        sl = slice(hd * LANES, (hd + 1) * LANES)
        q_ref[0, hd] = q[:, sl]
        k_ref[0, hd] = k[:, sl]
        vt_ref[0, hd, 0] = vt[sl, :]


def _in_proj(x, mod, prep, tm):
    B, S, _ = x.shape
    grid = (B, S // tm)
    row = lambda b, i: (b, i, 0)
    hrow = lambda b, i: (b, 0, i, 0)
    head_shape = jax.ShapeDtypeStruct((B, N_HEADS, S, LANES), BF16)
    hw = N_HEADS * LANES
    return pl.pallas_call(
        _in_proj_kernel,
        out_shape=(
            head_shape, head_shape,
            jax.ShapeDtypeStruct((B, N_HEADS, S // tm, LANES, tm), BF16),
            jax.ShapeDtypeStruct((B, S, 2 * D_MLSTM), BF16),
            jax.ShapeDtypeStruct((B, D_MLSTM, S), BF16),
            jax.ShapeDtypeStruct((B, S, D_MLSTM), F32),
            jax.ShapeDtypeStruct((B, S, LANES), F32),
        ),
        grid=grid,
        in_specs=[pl.BlockSpec((1, tm, D_MODEL), row)] + _halo_specs(tm, S, D_MODEL) + [
            pl.BlockSpec((1, 1, 6 * D_MODEL), lambda b, i: (b, 0, 0)),
            _const_spec((1, D_MODEL)),
            _const_spec((D_MODEL, _N_PREP)),
            _const_spec((1, LANES)),
            _const_spec((1, Q_RANK)),
            _const_spec((1, KV_RANK)),
            _const_spec((Q_RANK, 2 * hw)),
            _const_spec((KV_RANK, hw)),
            _const_spec((hw, KV_RANK)),
            pl.BlockSpec((tm, LANES), lambda b, i: (i, 0)),
            pl.BlockSpec((tm, LANES), lambda b, i: (i, 0)),
            _const_spec((hw, 1)),
            _const_spec((3, 2 * D_MLSTM)),
            _const_spec((1, 2 * D_MLSTM)),
            _const_spec((D_MLSTM, D_MODEL)),
        ],
        out_specs=(
            pl.BlockSpec((1, N_HEADS, tm, LANES), hrow),
            pl.BlockSpec((1, N_HEADS, tm, LANES), hrow),
            pl.BlockSpec((1, N_HEADS, 1, LANES, tm), lambda b, i: (b, 0, i, 0, 0)),
            pl.BlockSpec((1, tm, 2 * D_MLSTM), row),
            pl.BlockSpec((1, D_MLSTM, tm), lambda b, i: (b, 0, i)),
            pl.BlockSpec((1, tm, D_MLSTM), row),
            pl.BlockSpec((1, tm, LANES), row),
        ),
        compiler_params=_params(("parallel", "parallel")),
        name="in_proj",
    )(x, x, x, mod, prep["norm1_g"], prep["w_in"], prep["b_gate"], prep["q_norm_g"], prep["kv_norm_g"],
      prep["w_q"], prep["w_k"], prep["w_vt"], prep["cos"], prep["sin"], prep["v_one"],
      prep["conv_m_w"], prep["conv_m_b"], prep["w_vm_t"])


def _attn_kernel(q_ref, qn_ref, k_ref, vt_ref, o_ref, s_scr, mx_scr, *, sub):
    i = pl.program_id(2)
    hps = q_ref.shape[1]
    ck = vt_ref.shape[4]
    tk = sub * ck
    n = k_ref.shape[2] // tk

    def scores(hh, qt, c, slot):
        s = _dot(k_ref[0, hh, c * tk:(c + 1) * tk, :], qt)
        s_scr[hh, slot] = s
        return jnp.max(s, axis=0, keepdims=True)

    qts = [q_ref[0, hh].astype(F32).T.astype(BF16) for hh in range(hps)]

    @pl.when(i == 0)
    def _():
        for hh in range(hps):
            mx_scr[hh] = scores(hh, qts[hh], 0, 0)

    mx = [mx_scr[hh] for hh in range(hps)]
    m = [None] * hps
    acc = [None] * hps
    for c in range(n):
        for hh in range(hps):
            if c + 1 < n:
                mx_next = scores(hh, qts[hh], c + 1, (c + 1) % 2)
            else:
                mx_next = scores(hh, qn_ref[0, hh].astype(F32).T.astype(BF16), 0, 0)
            m_new = mx[hh] if c == 0 else jnp.maximum(m[hh], mx[hh])
            p = jnp.exp2(s_scr[hh, c % 2] - m_new).astype(BF16)
            pv = _dot(vt_ref[0, hh, c * sub, 0:PV_ROWS, :], p[0:ck])
            for u in range(1, sub):
                pv = pv + _dot(vt_ref[0, hh, c * sub + u, 0:PV_ROWS, :], p[u * ck:(u + 1) * ck])
            acc[hh] = pv if c == 0 else jnp.exp2(m[hh] - m_new) * acc[hh] + pv
            m[hh], mx[hh] = m_new, mx_next
    for hh in range(hps):
        mx_scr[hh] = mx[hh]
        out = acc[hh][0:D_V] / acc[hh][D_V:D_V + 1]
        o_ref[0, hh] = jnp.concatenate([out, jnp.zeros((LANES - D_V, out.shape[1]), F32)],
                                       axis=0).T.astype(BF16)


def _attention(q, k, vt, tq, sub):
    B, H, S, _ = q.shape
    ck = vt.shape[4]
    tk = sub * ck
    nq = S // tq
    assert S % (2 * tk) == 0 and S % tq == 0
    return pl.pallas_call(
        functools.partial(_attn_kernel, sub=sub),
        out_shape=jax.ShapeDtypeStruct((B, H, S, LANES), BF16),
        grid=(B, H // ATT_HPS, nq),
        in_specs=[
            pl.BlockSpec((1, ATT_HPS, tq, LANES), lambda b, h, i: (b, h, i, 0)),
            pl.BlockSpec((1, ATT_HPS, tq, LANES), lambda b, h, i: (b, h, jnp.minimum(i + 1, nq - 1), 0)),
            pl.BlockSpec((1, ATT_HPS, S, LANES), lambda b, h, i: (b, h, 0, 0)),
            pl.BlockSpec((1, ATT_HPS, S // ck, LANES, ck), lambda b, h, i: (b, h, 0, 0, 0)),
        ],
        out_specs=pl.BlockSpec((1, ATT_HPS, tq, LANES), lambda b, h, i: (b, h, i, 0)),
        scratch_shapes=[pltpu.VMEM((ATT_HPS, 2, tk, tq), F32), pltpu.VMEM((ATT_HPS, 1, tq), F32)],
        compiler_params=_params(("parallel", "parallel", "arbitrary")),
        name="attention",
    )(q, q, k, vt)


def _split3(x):
    hi = x.astype(BF16)
    r1 = x - hi.astype(F32)
    mid = r1.astype(BF16)
    lo = (r1 - mid.astype(F32)).astype(BF16)
    return hi, mid, lo


def _mlstm_direction(qk_ref, vt_ref, g_ref, h_ref, c_scr, m_scr, *, sub, reverse):
    L = CHUNK
    rs = slice(sub * L, (sub + 1) * L)
    g = g_ref[0, rs, :]
    gt = g.T
    si = lax.broadcasted_iota(jnp.int32, (L, L), 0)
    ti = lax.broadcasted_iota(jnp.int32, (L, L), 1)
    vis = (si >= ti) if reverse else (si <= ti)
    vis_bf = jnp.where(vis, 1.0, 0.0).astype(BF16)
    vis_t_bf = jnp.where((ti >= si) if reverse else (ti <= si), 1.0, 0.0).astype(BF16)
    b_cols = sum(_dot(vis_t_bf, piece) for piece in _split3(g))
    b_rows = sum(_dot(piece, vis_bf) for piece in _split3(gt))
    one_rows = jnp.where(lax.broadcasted_iota(jnp.int32, (ST_ROWS - M_DIM, L), 0) == 0, 1.0, 0.0).astype(BF16)

    i_off = 2 * M_HEADS if reverse else 0
    f_off = i_off + M_HEADS
    last = 0 if reverse else L - 1
    for hd in range(M_HEADS):
        sl = slice(hd * M_DIM, (hd + 1) * M_DIM)
        qb = qk_ref[0, rs, sl]
        kb = qk_ref[0, rs, D_MLSTM + hd * M_DIM:D_MLSTM + (hd + 1) * M_DIM]
        vt_ext = jnp.concatenate([vt_ref[0, sl, rs], one_rows], axis=0)
        b_r = b_rows[f_off + hd:f_off + hd + 1, :]
        li_r = gt[i_off + hd:i_off + hd + 1, :]
        u_c = g[:, i_off + hd:i_off + hd + 1] - b_cols[:, f_off + hd:f_off + hd + 1]
        m_prev = m_scr[hd]
        c_prev = c_scr[hd]

        dmat = jnp.where(vis, b_r + u_c, -jnp.inf)
        inter = b_r + m_prev
        m_t = jnp.maximum(inter, jnp.max(dmat, axis=0, keepdims=True))
        w_intra = jnp.exp(dmat - m_t)
        w_state = jnp.exp(inter - m_t)
        s = _dot_nt(kb, qb) * w_intra
        tot = _dot(vt_ext, s.astype(BF16)) + w_state * _dot_nt(c_prev.astype(BF16), qb)
        den = tot[M_DIM:M_DIM + 1, :]
        h_t = tot[0:M_DIM] / jnp.maximum(jnp.abs(den), jnp.exp(-m_t))
        h_ref[0, rs, sl] = h_t.T

        b_last = b_r[:, last:last + 1]
        a = b_last - b_r + li_r
        m_new = jnp.maximum(b_last + m_prev, jnp.max(a, axis=1, keepdims=True))
        w_s = jnp.exp(a - m_new)
        decay = jnp.exp(b_last + m_prev - m_new)
        vw = (vt_ext.astype(F32) * w_s).astype(BF16)
        c_scr[hd] = decay * c_prev + _dot(vw, kb)
        m_scr[hd] = m_new


def _mlstm_kernel(qkf_ref, vf_ref, gf_ref, qkb_ref, vb_ref, gb_ref,
                  hf_ref, hb_ref, cf_scr, mf_scr, cb_scr, mb_scr):
    @pl.when(pl.program_id(1) == 0)
    def _():
        for r in (cf_scr, mf_scr, cb_scr, mb_scr):
            r[...] = jnp.zeros_like(r)

    for j in range(M_CPS):
        _mlstm_direction(qkf_ref, vf_ref, gf_ref, hf_ref, cf_scr, mf_scr, sub=j, reverse=False)
        _mlstm_direction(qkb_ref, vb_ref, gb_ref, hb_ref, cb_scr, mb_scr, sub=M_CPS - 1 - j, reverse=True)


def _mlstm(qk_act, v_m, gates):
    B, S, _ = qk_act.shape
    rows = M_CPS * CHUNK
    nc = S // rows

    def specs(cidx):
        main = lambda b, i: (b, cidx(i), 0)
        return [
            pl.BlockSpec((1, rows, 2 * D_MLSTM), main),
            pl.BlockSpec((1, D_MLSTM, rows), lambda b, i: (b, 0, cidx(i))),
            pl.BlockSpec((1, rows, LANES), main),
        ]

    fwd = lambda i: i
    bwd = lambda i: nc - 1 - i
    state = [pltpu.VMEM((M_HEADS, ST_ROWS, M_DIM), F32), pltpu.VMEM((M_HEADS, 1, 1), F32)]
    h_shape = jax.ShapeDtypeStruct((B, S, D_MLSTM), F32)
    return pl.pallas_call(
        _mlstm_kernel,
        out_shape=(h_shape, h_shape),
        grid=(B, nc),
        in_specs=specs(fwd) + specs(bwd),
        out_specs=(
            pl.BlockSpec((1, rows, D_MLSTM), lambda b, i: (b, i, 0)),
            pl.BlockSpec((1, rows, D_MLSTM), lambda b, i: (b, nc - 1 - i, 0)),
        ),
        scratch_shapes=state + state,
        compiler_params=_params(("parallel", "arbitrary")),
        name="mlstm",
    )(qk_act, v_m, gates, qk_act, v_m, gates)


def _out_proj_kernel(x_ref, mod_ref, a_ref, hf_ref, hb_ref, om_ref, mg_ref, wa_ref, wm_ref, o_ref):
    g1 = mod_ref[0, :, 2 * D_MODEL:3 * D_MODEL]
    attn = jnp.concatenate([a_ref[0, hd] for hd in range(N_HEADS)], axis=1)
    y = _dot(attn, wa_ref[...])
    h = hf_ref[0] + hb_ref[0]
    om = om_ref[0]
    parts = []
    for hd in range(M_HEADS):
        sl = slice(hd * M_DIM, (hd + 1) * M_DIM)
        hh = h[:, sl]
        mu = jnp.mean(hh, axis=1, keepdims=True)
        d = hh - mu
        var = jnp.mean(d * d, axis=1, keepdims=True)
        parts.append(d * lax.rsqrt(var + EPS) * mg_ref[:, sl] * jax.nn.sigmoid(om[:, sl]))
    mem = jnp.concatenate(parts, axis=1).astype(BF16)
    y = y + _dot(mem, wm_ref[...])
    o_ref[0] = x_ref[0] + g1 * y


def _out_proj(x, mod, attn, h_f, h_b, o_m, prep, tm):
    B, S, _ = x.shape
    row = lambda b, i: (b, i, 0)
    return pl.pallas_call(
        _out_proj_kernel,
        out_shape=jax.ShapeDtypeStruct((B, S, D_MODEL), F32),
        grid=(B, S // tm),
        in_specs=[
            pl.BlockSpec((1, tm, D_MODEL), row),
            pl.BlockSpec((1, 1, 6 * D_MODEL), lambda b, i: (b, 0, 0)),
            pl.BlockSpec((1, N_HEADS, tm, LANES), lambda b, i: (b, 0, i, 0)),
            pl.BlockSpec((1, tm, D_MLSTM), row),
            pl.BlockSpec((1, tm, D_MLSTM), row),
            pl.BlockSpec((1, tm, D_MLSTM), row),
            _const_spec((1, D_MLSTM)),
            _const_spec((N_HEADS * LANES, D_MODEL)),
            _const_spec((D_MLSTM, D_MODEL)),
        ],
        out_specs=pl.BlockSpec((1, tm, D_MODEL), row),
        compiler_params=_params(("parallel", "parallel")),
        name="out_proj",
    )(x, mod, attn, h_f, h_b, o_m, prep["mh_norm_g"], prep["w_out_attn"], prep["w_out_mem"])


def _ffn_kernel(x_ref, xp_ref, xn_ref, mod_ref, n2_ref, wu_ref, cw_ref, cb_ref, wd_ref, fg_ref, o_ref):
    i = pl.program_id(1)
    nt = pl.num_programs(1)
    tm = x_ref.shape[1]
    sh2 = mod_ref[0, :, 3 * D_MODEL:4 * D_MODEL]
    sc2 = mod_ref[0, :, 4 * D_MODEL:5 * D_MODEL]
    g2 = mod_ref[0, :, 5 * D_MODEL:6 * D_MODEL]

    def norm_mod(r):
        return _rmsnorm(r, n2_ref[...]) * (1.0 + sc2) + sh2

    x = x_ref[0]
    hp = jnp.where(i > 0, norm_mod(xp_ref[0]), 0.0)
    hn = jnp.where(i < nt - 1, norm_mod(xn_ref[0]), 0.0)
    hext = jnp.concatenate([hp, norm_mod(x), hn], axis=0).astype(BF16)

    n_chunks = D_FF // FF_CHUNK
    ca = [slice(j * FF_CHUNK, (j + 1) * FF_CHUNK) for j in range(n_chunks)]
    cg = [slice(D_FF + j * FF_CHUNK, D_FF + (j + 1) * FF_CHUNK) for j in range(n_chunks)]

    def up(j):
        return _dot(hext, wu_ref[:, ca[j]]), _dot(hext, wu_ref[:, cg[j]])

    acc = jnp.zeros((tm, D_MODEL), F32)
    nxt = up(0)
    for j in range(n_chunks):
        ua, ug = nxt
        if j + 1 < n_chunks:
            nxt = up(j + 1)
        a = _conv3(ua, cw_ref, cb_ref, ca[j], tm)
        g = _conv3(ug, cw_ref, cb_ref, cg[j], tm)
        act = (g * jax.nn.sigmoid(g) * a).astype(BF16)
        acc = acc + _dot(act, wd_ref[ca[j], :])
    o_ref[0] = _rmsnorm(x + g2 * acc, fg_ref[...])


def _conv_ffn(x, mod, prep, tm):
    B, S, _ = x.shape
    row = lambda b, i: (b, i, 0)
    return pl.pallas_call(
        _ffn_kernel,
        out_shape=jax.ShapeDtypeStruct((B, S, D_MODEL), F32),
        grid=(B, S // tm),
        in_specs=[pl.BlockSpec((1, tm, D_MODEL), row)] + _halo_specs(tm, S, D_MODEL) + [
            pl.BlockSpec((1, 1, 6 * D_MODEL), lambda b, i: (b, 0, 0)),
            _const_spec((1, D_MODEL)),
            _const_spec((D_MODEL, 2 * D_FF)),
            _const_spec((3, 2 * D_FF)),
            _const_spec((1, 2 * D_FF)),
            _const_spec((D_FF, D_MODEL)),
            _const_spec((1, D_MODEL)),
        ],
        out_specs=pl.BlockSpec((1, tm, D_MODEL), row),
        compiler_params=_params(("parallel", "parallel")),
        name="conv_ffn",
    )(x, x, x, mod, prep["norm2_g"], prep["w_up"], prep["conv_f_w"], prep["conv_f_b"],
      prep["w_down"], prep["final_g"])


def _head_groups(w, n_heads, width, pieces):
    rows = w.shape[0]
    wh = w.reshape(rows, n_heads, width)
    out = jnp.zeros((rows, n_heads, LANES), F32)
    for src, n, dst, sign in pieces:
        out = out.at[:, :, dst:dst + n].set(sign * wh[:, :, src:src + n])
    return out.reshape(rows, n_heads * LANES)


def _prepare(S, norm1_g, w_in, b_gate, q_norm_g, kv_norm_g, w_uq, w_ukv, conv_m_w, conv_m_b,
             mh_norm_g, w_out, norm2_g, w_up, conv_f_w, conv_f_b, w_down, final_g):
    half = D_ROPE // 2
    r1 = D_NOPE
    r2 = D_NOPE + half
    c1, c2 = Q_RANK + KV_RANK, Q_RANK + KV_RANK + D_ROPE
    c3 = c2 + 2 * D_MLSTM
    c4 = c3 + D_MLSTM
    c5 = c4 + D_MLSTM

    kr = w_in[:, c1:c2]
    zeros = lambda n: jnp.zeros((D_MODEL, n), F32)
    kr_main = jnp.concatenate([zeros(r1), kr[:, :half], kr[:, half:], zeros(LANES - r2 - half)], axis=1)
    kr_swap = jnp.concatenate([zeros(r1), -kr[:, half:], kr[:, :half], zeros(LANES - r2 - half)], axis=1)
    gates = jnp.concatenate([w_in[:, c5:], zeros(LANES - 4 * M_HEADS)], axis=1)
    w_prep = jnp.concatenate([w_in[:, :c1], kr_main, kr_swap, w_in[:, c2:c3], w_in[:, c4:c5], gates], axis=1)

    qs = (D_NOPE + D_ROPE) ** -0.5 * math.log2(math.e)
    dq = D_NOPE + D_ROPE
    wq_main = _head_groups(w_uq, N_HEADS, dq, [(0, D_NOPE, 0, 1.0), (D_NOPE, half, r1, 1.0),
                                                (D_NOPE + half, half, r2, 1.0)])
    wq_swap = _head_groups(w_uq, N_HEADS, dq, [(D_NOPE + half, half, r1, -1.0), (D_NOPE, half, r2, 1.0)])
    w_q = jnp.concatenate([wq_main, wq_swap], axis=1) * qs

    dkv = D_NOPE + D_V
    wk = _head_groups(w_ukv, N_HEADS, dkv, [(0, D_NOPE, 0, 1.0)])
    wv = _head_groups(w_ukv, N_HEADS, dkv, [(D_NOPE, D_V, 0, 1.0)])

    inv = 1.0 / (ROPE_THETA ** (jnp.arange(half, dtype=F32) * (2.0 / D_ROPE)))
    ang = jnp.arange(S, dtype=jnp.int32).astype(F32)[:, None] * inv[None, :]
    ones = jnp.ones((S, r1), F32)
    pad = jnp.zeros((S, LANES - r2 - half), F32)
    cos = jnp.concatenate([ones, jnp.cos(ang), jnp.cos(ang), pad], axis=1)
    sin = jnp.concatenate([0.0 * ones, jnp.sin(ang), jnp.sin(ang), pad], axis=1)

    v_one = jnp.zeros((N_HEADS, LANES), F32).at[:, D_V].set(1.0).reshape(N_HEADS * LANES, 1)
    b_g = jnp.concatenate([b_gate, jnp.zeros((LANES - 4 * M_HEADS,), F32)]).reshape(1, LANES)

    d_attn = N_HEADS * D_V
    w_oa = jnp.zeros((N_HEADS, LANES, D_MODEL), F32).at[:, :D_V, :].set(
        w_out[:d_attn].reshape(N_HEADS, D_V, D_MODEL)).reshape(N_HEADS * LANES, D_MODEL)

    return dict(
        norm1_g=norm1_g.reshape(1, -1), w_in=w_prep.astype(BF16), w_vm_t=w_in[:, c3:c4].T.astype(BF16), b_gate=b_g,
        q_norm_g=q_norm_g.reshape(1, -1), kv_norm_g=kv_norm_g.reshape(1, -1),
        w_q=w_q.astype(BF16), w_k=wk.astype(BF16), w_vt=wv.T.astype(BF16), cos=cos, sin=sin, v_one=v_one,
        conv_m_w=conv_m_w, conv_m_b=conv_m_b.reshape(1, -1), mh_norm_g=mh_norm_g.reshape(1, -1),
        w_out_attn=w_oa.astype(BF16), w_out_mem=w_out[d_attn:].astype(BF16),
        norm2_g=norm2_g.reshape(1, -1), w_up=w_up.astype(BF16), conv_f_w=conv_f_w,
        conv_f_b=conv_f_b.reshape(1, -1), w_down=w_down.astype(BF16), final_g=final_g.reshape(1, -1),
    )


def _tiles(S):
    tm = min(512, S // 2)
    sub = 2 if S % (4 * tm) == 0 else 1
    return tm, min(512, S), sub


def _trunk(x, c, w_ada, b_ada, prep):
    S = x.shape[1]
    tm, tq, sub = _tiles(S)
    mod = _ada_mod(c, w_ada, b_ada)
    q, k, vt, qk_act, v_m, o_m, gates = _in_proj(x, mod, prep, tm)
    attn = _attention(q, k, vt, tq, sub)
    h_f, h_b = _mlstm(qk_act, v_m, gates)
    x1 = _out_proj(x, mod, attn, h_f, h_b, o_m, prep, tm)
    return _conv_ffn(x1, mod, prep, min(FFN_ROWS, tm))


def kernel(x_prompt, x_sample, c_prompt, c_sample, norm1_g, w_ada, b_ada, w_in, b_gate, q_norm_g, kv_norm_g, w_uq, w_ukv, conv_m_w, conv_m_b, mh_norm_g, w_out, norm2_g, w_up, conv_f_w, conv_f_b, w_down, final_g):
    assert w_ada.shape[0] == 1, "single-layer trunk"
    assert x_prompt.shape[1] == x_sample.shape[1]
    prep = _prepare(x_prompt.shape[1], norm1_g[0], w_in[0], b_gate[0], q_norm_g[0], kv_norm_g[0], w_uq[0],
                    w_ukv[0], conv_m_w[0], conv_m_b[0], mh_norm_g[0], w_out[0], norm2_g[0], w_up[0],
                    conv_f_w[0], conv_f_b[0], w_down[0], final_g)
    y_prompt = _trunk(x_prompt, c_prompt, w_ada[0], b_ada[0], prep)
    y_sample = _trunk(x_sample, c_sample, w_ada[0], b_ada[0], prep)
    return (y_prompt, y_sample)
```

```python
import functools
import math

import jax
import jax.numpy as jnp
from jax import lax
from jax.experimental import pallas as pl
from jax.experimental.pallas import tpu as pltpu

F32 = jnp.float32
BF16 = jnp.bfloat16

D_MODEL = 1024
N_HEADS = 8
D_NOPE = 64
D_ROPE = 32
D_V = 64
Q_RANK = 384
KV_RANK = 256
M_HEADS = 4
M_DIM = 128
D_MLSTM = M_HEADS * M_DIM
CHUNK = 128
M_CPS = 4
D_FF = 2816
ROPE_THETA = 10000.0
EPS = 1e-6
LANES = 128
SUBLANES = 8
FF_CHUNK = 256
FFN_ROWS = 256
ST_ROWS = M_DIM + 16
ATT_HPS = 2
SCORE_LIMIT = 40.0
VALUE_LIMIT = 2.0 ** 60
PV_ROWS = 80
VMEM_LIMIT = 56 * 1024 * 1024

_A_W = Q_RANK + KV_RANK + 2 * LANES
_QK0 = _A_W
_O0 = _QK0 + 2 * D_MLSTM
_G0 = _O0 + D_MLSTM
_N_PREP = _G0 + LANES


def _dot(a, b):
    return jnp.dot(a, b, preferred_element_type=F32)


def _dot_nt(a, b):
    return lax.dot_general(a, b, (((1,), (1,)), ((), ())), preferred_element_type=F32)


def _dot_tn(a, b):
    return lax.dot_general(a, b, (((0,), (0,)), ((), ())), preferred_element_type=F32)


def _rmsnorm(x, g):
    return x * lax.rsqrt(jnp.mean(x * x, axis=-1, keepdims=True) + EPS) * g


def _log_sigmoid(x):
    return jnp.minimum(x, 0.0) - jnp.log1p(jnp.exp(-jnp.abs(x)))


def _params(sem):
    return pltpu.CompilerParams(dimension_semantics=sem, vmem_limit_bytes=VMEM_LIMIT)


def _const_spec(shape):
    nd = len(shape)
    return pl.BlockSpec(shape, lambda *_: (0,) * nd, pipeline_mode=pl.Buffered(1))


def _halo_specs(tm, S, width):
    per = tm // SUBLANES
    nsub = S // SUBLANES
    return [
        pl.BlockSpec((1, SUBLANES, width), lambda b, i: (b, jnp.maximum(i * per - 1, 0), 0)),
        pl.BlockSpec((1, SUBLANES, width), lambda b, i: (b, jnp.minimum((i + 1) * per, nsub - 1), 0)),
    ]


def _conv3(u, cw_ref, cb_ref, cols, tm):
    rows = tm + 2 * SUBLANES
    full = (cw_ref[0:1, cols] * pltpu.roll(u, 1, axis=0) + cw_ref[1:2, cols] * u
            + cw_ref[2:3, cols] * pltpu.roll(u, rows - 1, axis=0) + cb_ref[:, cols])
    return full[SUBLANES:SUBLANES + tm]


def _ada_kernel(c_ref, w_ref, b_ref, o_ref):
    c = c_ref[...]
    a = (c * jax.nn.sigmoid(c)).astype(BF16)
    o_ref[...] = _dot(a, w_ref[...].astype(BF16)) + b_ref[...]


def _ada_mod(c, w_ada, b_ada):
    B = c.shape[0]
    cp = jnp.zeros((SUBLANES, D_MODEL), F32).at[:B].set(c)
    n = 6 * D_MODEL
    out = pl.pallas_call(
        _ada_kernel,
        out_shape=jax.ShapeDtypeStruct((SUBLANES, n), F32),
        grid=(6,),
        in_specs=[
            pl.BlockSpec((SUBLANES, D_MODEL), lambda j: (0, 0)),
            pl.BlockSpec((D_MODEL, D_MODEL), lambda j: (0, j)),
            pl.BlockSpec((1, D_MODEL), lambda j: (0, j)),
        ],
        out_specs=pl.BlockSpec((SUBLANES, D_MODEL), lambda j: (0, j)),
        compiler_params=_params(("parallel",)),
        name="ada_mod",
    )(cp, w_ada, b_ada.reshape(1, n))
    return out[:B].reshape(B, 1, n)


def _in_proj_kernel(x_ref, xp_ref, xn_ref, mod_ref, n1_ref, w_ref, bg_ref, qn_ref, kvn_ref, wq_ref, wk_ref,
                    wvt_ref, cos_ref, sin_ref, one_ref, cw_ref, cb_ref, wvm_ref,
                    q_ref, k_ref, vt_ref, qk_ref, vm_ref, om_ref, g_ref, st_ref):
    i = pl.program_id(1)
    nt = pl.num_programs(1)
    tm = x_ref.shape[1]
    sh1 = mod_ref[0, :, 0:D_MODEL]
    sc1 = mod_ref[0, :, D_MODEL:2 * D_MODEL]

    def norm_mod(r):
        return _rmsnorm(r, n1_ref[...]) * (1.0 + sc1) + sh1

    hf = norm_mod(x_ref[0])
    h = hf.astype(BF16)
    hp = jnp.where(i > 0, norm_mod(xp_ref[0]), 0.0)
    hn = jnp.where(i < nt - 1, norm_mod(xn_ref[0]), 0.0)
    hext = jnp.concatenate([hp, hf, hn], axis=0).astype(BF16)
    conv = _conv3(_dot(hext, w_ref[:, _QK0:_O0]), cw_ref, cb_ref, slice(None), tm)
    qk = conv * jax.nn.sigmoid(conv)
    qk_ref[0, :, 0:D_MLSTM] = (qk[:, 0:D_MLSTM] * (M_DIM ** -0.5)).astype(BF16)
    qk_ref[0, :, D_MLSTM:] = qk[:, D_MLSTM:].astype(BF16)

    pa = _dot(h, w_ref[:, 0:_A_W])
    vm_ref[0] = _dot_nt(wvm_ref[...], h).astype(BF16)
    om_ref[0] = _dot(h, w_ref[:, _O0:_G0])
    g = _dot(h, w_ref[:, _G0:_N_PREP]) + bg_ref[...]
    lane = lax.broadcasted_iota(jnp.int32, g.shape, 1)
    is_forget = (lane < 4 * M_HEADS) & ((lane // M_HEADS) % 2 == 1)
    g_ref[0] = jnp.where(is_forget, _log_sigmoid(g), g)

    cos = cos_ref[...]
    sin = sin_ref[...]
    cos8 = jnp.concatenate([cos] * N_HEADS, axis=1)
    sin8 = jnp.concatenate([sin] * N_HEADS, axis=1)
    hw = N_HEADS * LANES

    cqn = _rmsnorm(pa[:, 0:Q_RANK], qn_ref[...]).astype(BF16)
    qq = _dot(cqn, wq_ref[...])
    q = (qq[:, 0:hw] * cos8 + qq[:, hw:2 * hw] * sin8).astype(BF16)

    ckvn = _rmsnorm(pa[:, Q_RANK:Q_RANK + KV_RANK], kvn_ref[...]).astype(BF16)
    kk = _dot(ckvn, wk_ref[...])
    kr0 = Q_RANK + KV_RANK
    kr = pa[:, kr0:kr0 + LANES] * cos + pa[:, kr0 + LANES:kr0 + 2 * LANES] * sin
    k = (kk + jnp.concatenate([kr] * N_HEADS, axis=1)).astype(BF16)
    vt = (_dot_nt(wvt_ref[...], ckvn) + one_ref[...]).astype(BF16)
    for hd in range(N_HEADS):
        sl = slice(hd * LANES, (hd + 1) * LANES)
        q_ref[0, hd] = q[:, sl]
        k_ref[0, hd] = k[:, sl]
        vt_ref[0, hd, 0] = vt[sl, :]
        kh = k[:, sl].astype(F32)
        k2 = jnp.max(jnp.sum(kh * kh, axis=1, keepdims=True), axis=0, keepdims=True)
        va = jnp.max(jnp.max(jnp.abs(vt[sl, :].astype(F32)), axis=1, keepdims=True), axis=0, keepdims=True)
        st_ref[0, 0, hd:hd + 1, :] = jnp.broadcast_to(k2, (1, LANES))
        st_ref[0, 0, N_HEADS + hd:N_HEADS + hd + 1, :] = jnp.broadcast_to(va, (1, LANES))


def _in_proj(x, mod, prep, tm):
    B, S, _ = x.shape
    grid = (B, S // tm)
    row = lambda b, i: (b, i, 0)
    hrow = lambda b, i: (b, 0, i, 0)
    head_shape = jax.ShapeDtypeStruct((B, N_HEADS, S, LANES), BF16)
    hw = N_HEADS * LANES
    return pl.pallas_call(
        _in_proj_kernel,
        out_shape=(
            head_shape, head_shape,
            jax.ShapeDtypeStruct((B, N_HEADS, S // tm, LANES, tm), BF16),
            jax.ShapeDtypeStruct((B, S, 2 * D_MLSTM), BF16),
            jax.ShapeDtypeStruct((B, D_MLSTM, S), BF16),
            jax.ShapeDtypeStruct((B, S, D_MLSTM), F32),
            jax.ShapeDtypeStruct((B, S, LANES), F32),
            jax.ShapeDtypeStruct((B, S // tm, 2 * N_HEADS, LANES), F32),
        ),
        grid=grid,
        in_specs=[pl.BlockSpec((1, tm, D_MODEL), row)] + _halo_specs(tm, S, D_MODEL) + [
            pl.BlockSpec((1, 1, 6 * D_MODEL), lambda b, i: (b, 0, 0)),
            _const_spec((1, D_MODEL)),
            _const_spec((D_MODEL, _N_PREP)),
            _const_spec((1, LANES)),
            _const_spec((1, Q_RANK)),
            _const_spec((1, KV_RANK)),
            _const_spec((Q_RANK, 2 * hw)),
            _const_spec((KV_RANK, hw)),
            _const_spec((hw, KV_RANK)),
            pl.BlockSpec((tm, LANES), lambda b, i: (i, 0)),
            pl.BlockSpec((tm, LANES), lambda b, i: (i, 0)),
            _const_spec((hw, 1)),
            _const_spec((3, 2 * D_MLSTM)),
            _const_spec((1, 2 * D_MLSTM)),
            _const_spec((D_MLSTM, D_MODEL)),
        ],
        out_specs=(
            pl.BlockSpec((1, N_HEADS, tm, LANES), hrow),
            pl.BlockSpec((1, N_HEADS, tm, LANES), hrow),
            pl.BlockSpec((1, N_HEADS, 1, LANES, tm), lambda b, i: (b, 0, i, 0, 0)),
            pl.BlockSpec((1, tm, 2 * D_MLSTM), row),
            pl.BlockSpec((1, D_MLSTM, tm), lambda b, i: (b, 0, i)),
            pl.BlockSpec((1, tm, D_MLSTM), row),
            pl.BlockSpec((1, tm, LANES), row),
            pl.BlockSpec((1, 1, 2 * N_HEADS, LANES), lambda b, i: (b, i, 0, 0)),
        ),
        compiler_params=_params(("parallel", "parallel")),
        name="in_proj",
    )(x, x, x, mod, prep["norm1_g"], prep["w_in"], prep["b_gate"], prep["q_norm_g"], prep["kv_norm_g"],
      prep["w_q"], prep["w_k"], prep["w_vt"], prep["cos"], prep["sin"], prep["v_one"],
      prep["conv_m_w"], prep["conv_m_b"], prep["w_vm_t"])


def _attn_kernel(thr_ref, q_ref, k_ref, vt_ref, o_ref, s_scr, *, sub):
    b = pl.program_id(0)
    hg = pl.program_id(1)
    hps = q_ref.shape[1]
    tq = q_ref.shape[2]
    ck = vt_ref.shape[4]
    tk = sub * ck
    n = k_ref.shape[2] // tk

    qtf = [q_ref[0, hh].astype(F32).T for hh in range(hps)]
    qts = [x.astype(BF16) for x in qtf]
    safe = None
    for hh in range(hps):
        q2 = jnp.max(jnp.sum(qtf[hh] * qtf[hh], axis=0, keepdims=True))
        ok = q2 <= thr_ref[b, hg * hps + hh]
        safe = ok if safe is None else jnp.logical_and(safe, ok)

    def value_dot(hh, c, p, acc=None):
        for u in range(sub):
            pv = _dot(vt_ref[0, hh, c * sub + u, 0:PV_ROWS, :], p[u * ck:(u + 1) * ck])
            acc = pv if acc is None else pv + acc
        return acc

    def store(hh, acc):
        out = acc[0:D_V] / acc[D_V:D_V + 1]
        o_ref[0, hh] = jnp.concatenate([out, jnp.zeros((LANES - D_V, tq), F32)], axis=0).T.astype(BF16)

    @pl.when(safe)
    def _():
        def scores(hh, c):
            return _dot(k_ref[0, hh, c * tk:(c + 1) * tk, :], qts[hh])

        acc = [None] * hps
        s_cur = [scores(hh, 0) for hh in range(hps)]
        for c in range(n):
            for hh in range(hps):
                s = s_cur[hh]
                if c + 1 < n:
                    s_cur[hh] = scores(hh, c + 1)
                acc[hh] = value_dot(hh, c, jnp.exp2(s).astype(BF16), acc[hh])
        for hh in range(hps):
            store(hh, acc[hh])

    @pl.when(jnp.logical_not(safe))
    def _():
        for hh in range(hps):
            def scores(c, slot, hh=hh):
                off = pl.multiple_of(c * tk, tk)
                s = _dot(k_ref[0, hh, pl.ds(off, tk), :], qts[hh])
                s_scr[slot] = s
                return jnp.max(s, axis=0, keepdims=True)

            def update(c, slot, mx, m, acc, hh=hh):
                m_new = jnp.maximum(m, mx)
                p = jnp.exp2(s_scr[slot] - m_new).astype(BF16)
                return m_new, jnp.exp2(m - m_new) * acc + value_dot(hh, c, p)

            def body(jj, carry):
                m, acc, mx_a = carry
                c = 2 * jj
                mx_b = scores(c + 1, 1)
                m, acc = update(c, 0, mx_a, m, acc)
                mx_a = scores(jnp.minimum(c + 2, n - 1), 0)
                m, acc = update(c + 1, 1, mx_b, m, acc)
                return m, acc, mx_a

            init = (jnp.full((1, tq), -jnp.inf, F32), jnp.zeros((PV_ROWS, tq), F32), scores(0, 0))
            _, acc, _ = lax.fori_loop(0, n // 2, body, init)
            store(hh, acc)


def _attention(q, k, vt, stats, tq, sub):
    B, H, S, _ = q.shape
    ck = vt.shape[4]
    tk = sub * ck
    assert S % (2 * tk) == 0 and S % tq == 0
    kmax2 = jnp.max(stats[:, :, 0:N_HEADS, 0], axis=1)
    vmax = jnp.max(stats[:, :, N_HEADS:, 0], axis=1)
    thr = jnp.where(vmax <= VALUE_LIMIT, SCORE_LIMIT ** 2 / jnp.maximum(kmax2, 1e-30), -1.0)
    grid_spec = pltpu.PrefetchScalarGridSpec(
        num_scalar_prefetch=1,
        grid=(B, H // ATT_HPS, S // tq),
        in_specs=[
            pl.BlockSpec((1, ATT_HPS, tq, LANES), lambda b, h, i, thr: (b, h, i, 0)),
            pl.BlockSpec((1, ATT_HPS, S, LANES), lambda b, h, i, thr: (b, h, 0, 0)),
            pl.BlockSpec((1, ATT_HPS, S // ck, LANES, ck), lambda b, h, i, thr: (b, h, 0, 0, 0)),
        ],
        out_specs=pl.BlockSpec((1, ATT_HPS, tq, LANES), lambda b, h, i, thr: (b, h, i, 0)),
        scratch_shapes=[pltpu.VMEM((2, tk, tq), F32)],
    )
    return pl.pallas_call(
        functools.partial(_attn_kernel, sub=sub),
        out_shape=jax.ShapeDtypeStruct((B, H, S, LANES), BF16),
        grid_spec=grid_spec,
        compiler_params=_params(("parallel", "parallel", "parallel")),
        name="attention",
    )(thr, q, k, vt)


def _split3(x):
    hi = x.astype(BF16)
    r1 = x - hi.astype(F32)
    mid = r1.astype(BF16)
    lo = (r1 - mid.astype(F32)).astype(BF16)
    return hi, mid, lo


def _mlstm_direction(qk_ref, vt_ref, g_ref, h_ref, c_scr, m_scr, *, sub, reverse):
    L = CHUNK
    rs = slice(sub * L, (sub + 1) * L)
    g = g_ref[0, rs, :]
    gt = g.T
    si = lax.broadcasted_iota(jnp.int32, (L, L), 0)
    ti = lax.broadcasted_iota(jnp.int32, (L, L), 1)
    vis = (si >= ti) if reverse else (si <= ti)
    vis_bf = jnp.where(vis, 1.0, 0.0).astype(BF16)
    vis_t_bf = jnp.where((ti >= si) if reverse else (ti <= si), 1.0, 0.0).astype(BF16)
    b_cols = sum(_dot(vis_t_bf, piece) for piece in _split3(g))
    b_rows = sum(_dot(piece, vis_bf) for piece in _split3(gt))
    one_rows = jnp.where(lax.broadcasted_iota(jnp.int32, (ST_ROWS - M_DIM, L), 0) == 0, 1.0, 0.0).astype(BF16)

    i_off = 2 * M_HEADS if reverse else 0
    f_off = i_off + M_HEADS
    last = 0 if reverse else L - 1
    for hd in range(M_HEADS):
        sl = slice(hd * M_DIM, (hd + 1) * M_DIM)
        qb = qk_ref[0, rs, sl]
        kb = qk_ref[0, rs, D_MLSTM + hd * M_DIM:D_MLSTM + (hd + 1) * M_DIM]
        vt_ext = jnp.concatenate([vt_ref[0, sl, rs], one_rows], axis=0)
        b_r = b_rows[f_off + hd:f_off + hd + 1, :]
        li_r = gt[i_off + hd:i_off + hd + 1, :]
        u_c = g[:, i_off + hd:i_off + hd + 1] - b_cols[:, f_off + hd:f_off + hd + 1]
        m_prev = m_scr[hd]
        c_prev = c_scr[hd]

        dmat = jnp.where(vis, b_r + u_c, -jnp.inf)
        inter = b_r + m_prev
        m_t = jnp.maximum(inter, jnp.max(dmat, axis=0, keepdims=True))
        w_intra = jnp.exp(dmat - m_t)
        w_state = jnp.exp(inter - m_t)
        s = _dot_nt(kb, qb) * w_intra
        tot = _dot(vt_ext, s.astype(BF16)) + w_state * _dot_nt(c_prev.astype(BF16), qb)
        den = tot[M_DIM:M_DIM + 1, :]
        h_t = tot[0:M_DIM] / jnp.maximum(jnp.abs(den), jnp.exp(-m_t))
        h_ref[0, rs, sl] = h_t.T

        b_last = b_r[:, last:last + 1]
        a = b_last - b_r + li_r
        m_new = jnp.maximum(b_last + m_prev, jnp.max(a, axis=1, keepdims=True))
        w_s = jnp.exp(a - m_new)
        decay = jnp.exp(b_last + m_prev - m_new)
        vw = (vt_ext.astype(F32) * w_s).astype(BF16)
        c_scr[hd] = decay * c_prev + _dot(vw, kb)
        m_scr[hd] = m_new


def _mlstm_kernel(qkf_ref, vf_ref, gf_ref, qkb_ref, vb_ref, gb_ref,
                  hf_ref, hb_ref, cf_scr, mf_scr, cb_scr, mb_scr):
    @pl.when(pl.program_id(1) == 0)
    def _():
        for r in (cf_scr, mf_scr, cb_scr, mb_scr):
            r[...] = jnp.zeros_like(r)

    for j in range(M_CPS):
        _mlstm_direction(qkf_ref, vf_ref, gf_ref, hf_ref, cf_scr, mf_scr, sub=j, reverse=False)
        _mlstm_direction(qkb_ref, vb_ref, gb_ref, hb_ref, cb_scr, mb_scr, sub=M_CPS - 1 - j, reverse=True)


def _mlstm(qk_act, v_m, gates):
    B, S, _ = qk_act.shape
    rows = M_CPS * CHUNK
    nc = S // rows

    def specs(cidx):
        main = lambda b, i: (b, cidx(i), 0)
        return [
            pl.BlockSpec((1, rows, 2 * D_MLSTM), main),
            pl.BlockSpec((1, D_MLSTM, rows), lambda b, i: (b, 0, cidx(i))),
            pl.BlockSpec((1, rows, LANES), main),
        ]

    fwd = lambda i: i
    bwd = lambda i: nc - 1 - i
    state = [pltpu.VMEM((M_HEADS, ST_ROWS, M_DIM), F32), pltpu.VMEM((M_HEADS, 1, 1), F32)]
    h_shape = jax.ShapeDtypeStruct((B, S, D_MLSTM), F32)
    return pl.pallas_call(
        _mlstm_kernel,
        out_shape=(h_shape, h_shape),
        grid=(B, nc),
        in_specs=specs(fwd) + specs(bwd),
        out_specs=(
            pl.BlockSpec((1, rows, D_MLSTM), lambda b, i: (b, i, 0)),
            pl.BlockSpec((1, rows, D_MLSTM), lambda b, i: (b, nc - 1 - i, 0)),
        ),
        scratch_shapes=state + state,
        compiler_params=_params(("parallel", "arbitrary")),
        name="mlstm",
    )(qk_act, v_m, gates, qk_act, v_m, gates)


def _out_proj_kernel(x_ref, mod_ref, a_ref, hf_ref, hb_ref, om_ref, mg_ref, wa_ref, wm_ref, o_ref):
    g1 = mod_ref[0, :, 2 * D_MODEL:3 * D_MODEL]
    attn = jnp.concatenate([a_ref[0, hd] for hd in range(N_HEADS)], axis=1)
    y = _dot(attn, wa_ref[...])
    h = hf_ref[0] + hb_ref[0]
    om = om_ref[0]
    parts = []
    for hd in range(M_HEADS):
        sl = slice(hd * M_DIM, (hd + 1) * M_DIM)
        hh = h[:, sl]
        mu = jnp.mean(hh, axis=1, keepdims=True)
        d = hh - mu
        var = jnp.mean(d * d, axis=1, keepdims=True)
        parts.append(d * lax.rsqrt(var + EPS) * mg_ref[:, sl] * jax.nn.sigmoid(om[:, sl]))
    mem = jnp.concatenate(parts, axis=1).astype(BF16)
    y = y + _dot(mem, wm_ref[...])
    o_ref[0] = x_ref[0] + g1 * y


def _out_proj(x, mod, attn, h_f, h_b, o_m, prep, tm):
    B, S, _ = x.shape
    row = lambda b, i: (b, i, 0)
    return pl.pallas_call(
        _out_proj_kernel,
        out_shape=jax.ShapeDtypeStruct((B, S, D_MODEL), F32),
        grid=(B, S // tm),
        in_specs=[
            pl.BlockSpec((1, tm, D_MODEL), row),
            pl.BlockSpec((1, 1, 6 * D_MODEL), lambda b, i: (b, 0, 0)),
            pl.BlockSpec((1, N_HEADS, tm, LANES), lambda b, i: (b, 0, i, 0)),
            pl.BlockSpec((1, tm, D_MLSTM), row),
            pl.BlockSpec((1, tm, D_MLSTM), row),
            pl.BlockSpec((1, tm, D_MLSTM), row),
            _const_spec((1, D_MLSTM)),
            _const_spec((N_HEADS * LANES, D_MODEL)),
            _const_spec((D_MLSTM, D_MODEL)),
        ],
        out_specs=pl.BlockSpec((1, tm, D_MODEL), row),
        compiler_params=_params(("parallel", "parallel")),
        name="out_proj",
    )(x, mod, attn, h_f, h_b, o_m, prep["mh_norm_g"], prep["w_out_attn"], prep["w_out_mem"])


def _ffn_kernel(x_ref, xp_ref, xn_ref, mod_ref, n2_ref, wu_ref, cw_ref, cb_ref, wd_ref, fg_ref, o_ref):
    i = pl.program_id(1)
    nt = pl.num_programs(1)
    tm = x_ref.shape[1]
    sh2 = mod_ref[0, :, 3 * D_MODEL:4 * D_MODEL]
    sc2 = mod_ref[0, :, 4 * D_MODEL:5 * D_MODEL]
    g2 = mod_ref[0, :, 5 * D_MODEL:6 * D_MODEL]

    def norm_mod(r):
        return _rmsnorm(r, n2_ref[...]) * (1.0 + sc2) + sh2

    x = x_ref[0]
    hp = jnp.where(i > 0, norm_mod(xp_ref[0]), 0.0)
    hn = jnp.where(i < nt - 1, norm_mod(xn_ref[0]), 0.0)
    hext = jnp.concatenate([hp, norm_mod(x), hn], axis=0).astype(BF16)

    n_chunks = D_FF // FF_CHUNK
    ca = [slice(j * FF_CHUNK, (j + 1) * FF_CHUNK) for j in range(n_chunks)]
    cg = [slice(D_FF + j * FF_CHUNK, D_FF + (j + 1) * FF_CHUNK) for j in range(n_chunks)]

    def up(j):
        return _dot(hext, wu_ref[:, ca[j]]), _dot(hext, wu_ref[:, cg[j]])

    acc = jnp.zeros((tm, D_MODEL), F32)
    nxt = up(0)
    for j in range(n_chunks):
        ua, ug = nxt
        if j + 1 < n_chunks:
            nxt = up(j + 1)
        a = _conv3(ua, cw_ref, cb_ref, ca[j], tm)
        g = _conv3(ug, cw_ref, cb_ref, cg[j], tm)
        act = (g * jax.nn.sigmoid(g) * a).astype(BF16)
        acc = acc + _dot(act, wd_ref[ca[j], :])
    o_ref[0] = _rmsnorm(x + g2 * acc, fg_ref[...])


def _conv_ffn(x, mod, prep, tm):
    B, S, _ = x.shape
    row = lambda b, i: (b, i, 0)
    return pl.pallas_call(
        _ffn_kernel,
        out_shape=jax.ShapeDtypeStruct((B, S, D_MODEL), F32),
        grid=(B, S // tm),
        in_specs=[pl.BlockSpec((1, tm, D_MODEL), row)] + _halo_specs(tm, S, D_MODEL) + [
            pl.BlockSpec((1, 1, 6 * D_MODEL), lambda b, i: (b, 0, 0)),
            _const_spec((1, D_MODEL)),
            _const_spec((D_MODEL, 2 * D_FF)),
            _const_spec((3, 2 * D_FF)),
            _const_spec((1, 2 * D_FF)),
            _const_spec((D_FF, D_MODEL)),
            _const_spec((1, D_MODEL)),
        ],
        out_specs=pl.BlockSpec((1, tm, D_MODEL), row),
        compiler_params=_params(("parallel", "parallel")),
        name="conv_ffn",
    )(x, x, x, mod, prep["norm2_g"], prep["w_up"], prep["conv_f_w"], prep["conv_f_b"],
      prep["w_down"], prep["final_g"])


def _head_groups(w, n_heads, width, pieces):
    rows = w.shape[0]
    wh = w.reshape(rows, n_heads, width)
    out = jnp.zeros((rows, n_heads, LANES), F32)
    for src, n, dst, sign in pieces:
        out = out.at[:, :, dst:dst + n].set(sign * wh[:, :, src:src + n])
    return out.reshape(rows, n_heads * LANES)


def _prepare(S, norm1_g, w_in, b_gate, q_norm_g, kv_norm_g, w_uq, w_ukv, conv_m_w, conv_m_b,
             mh_norm_g, w_out, norm2_g, w_up, conv_f_w, conv_f_b, w_down, final_g):
    half = D_ROPE // 2
    r1 = D_NOPE
    r2 = D_NOPE + half
    c1, c2 = Q_RANK + KV_RANK, Q_RANK + KV_RANK + D_ROPE
    c3 = c2 + 2 * D_MLSTM
    c4 = c3 + D_MLSTM
    c5 = c4 + D_MLSTM

    kr = w_in[:, c1:c2]
    zeros = lambda n: jnp.zeros((D_MODEL, n), F32)
    kr_main = jnp.concatenate([zeros(r1), kr[:, :half], kr[:, half:], zeros(LANES - r2 - half)], axis=1)
    kr_swap = jnp.concatenate([zeros(r1), -kr[:, half:], kr[:, :half], zeros(LANES - r2 - half)], axis=1)
    gates = jnp.concatenate([w_in[:, c5:], zeros(LANES - 4 * M_HEADS)], axis=1)
    w_prep = jnp.concatenate([w_in[:, :c1], kr_main, kr_swap, w_in[:, c2:c3], w_in[:, c4:c5], gates], axis=1)

    qs = (D_NOPE + D_ROPE) ** -0.5 * math.log2(math.e)
    dq = D_NOPE + D_ROPE
    wq_main = _head_groups(w_uq, N_HEADS, dq, [(0, D_NOPE, 0, 1.0), (D_NOPE, half, r1, 1.0),
                                                (D_NOPE + half, half, r2, 1.0)])
    wq_swap = _head_groups(w_uq, N_HEADS, dq, [(D_NOPE + half, half, r1, -1.0), (D_NOPE, half, r2, 1.0)])
    w_q = jnp.concatenate([wq_main, wq_swap], axis=1) * qs

    dkv = D_NOPE + D_V
    wk = _head_groups(w_ukv, N_HEADS, dkv, [(0, D_NOPE, 0, 1.0)])
    wv = _head_groups(w_ukv, N_HEADS, dkv, [(D_NOPE, D_V, 0, 1.0)])

    inv = 1.0 / (ROPE_THETA ** (jnp.arange(half, dtype=F32) * (2.0 / D_ROPE)))
    ang = jnp.arange(S, dtype=jnp.int32).astype(F32)[:, None] * inv[None, :]
    ones = jnp.ones((S, r1), F32)
    pad = jnp.zeros((S, LANES - r2 - half), F32)
    cos = jnp.concatenate([ones, jnp.cos(ang), jnp.cos(ang), pad], axis=1)
    sin = jnp.concatenate([0.0 * ones, jnp.sin(ang), jnp.sin(ang), pad], axis=1)

    v_one = jnp.zeros((N_HEADS, LANES), F32).at[:, D_V].set(1.0).reshape(N_HEADS * LANES, 1)
    b_g = jnp.concatenate([b_gate, jnp.zeros((LANES - 4 * M_HEADS,), F32)]).reshape(1, LANES)

    d_attn = N_HEADS * D_V
    w_oa = jnp.zeros((N_HEADS, LANES, D_MODEL), F32).at[:, :D_V, :].set(
        w_out[:d_attn].reshape(N_HEADS, D_V, D_MODEL)).reshape(N_HEADS * LANES, D_MODEL)

    return dict(
        norm1_g=norm1_g.reshape(1, -1), w_in=w_prep.astype(BF16), w_vm_t=w_in[:, c3:c4].T.astype(BF16), b_gate=b_g,
        q_norm_g=q_norm_g.reshape(1, -1), kv_norm_g=kv_norm_g.reshape(1, -1),
        w_q=w_q.astype(BF16), w_k=wk.astype(BF16), w_vt=wv.T.astype(BF16), cos=cos, sin=sin, v_one=v_one,
        conv_m_w=conv_m_w, conv_m_b=conv_m_b.reshape(1, -1), mh_norm_g=mh_norm_g.reshape(1, -1),
        w_out_attn=w_oa.astype(BF16), w_out_mem=w_out[d_attn:].astype(BF16),
        norm2_g=norm2_g.reshape(1, -1), w_up=w_up.astype(BF16), conv_f_w=conv_f_w,
        conv_f_b=conv_f_b.reshape(1, -1), w_down=w_down.astype(BF16), final_g=final_g.reshape(1, -1),
    )


def _tiles(S):
    tm = min(512, S // 2)
    sub = 2 if S % (4 * tm) == 0 else 1
    return tm, min(512, S), sub


def _trunk(x, c, w_ada, b_ada, prep):
    S = x.shape[1]
    tm, tq, sub = _tiles(S)
    mod = _ada_mod(c, w_ada, b_ada)
    q, k, vt, qk_act, v_m, o_m, gates, stats = _in_proj(x, mod, prep, tm)
    attn = _attention(q, k, vt, stats, tq, sub)
    h_f, h_b = _mlstm(qk_act, v_m, gates)
    x1 = _out_proj(x, mod, attn, h_f, h_b, o_m, prep, tm)
    return _conv_ffn(x1, mod, prep, min(FFN_ROWS, tm))


def kernel(x_prompt, x_sample, c_prompt, c_sample, norm1_g, w_ada, b_ada, w_in, b_gate, q_norm_g, kv_norm_g, w_uq, w_ukv, conv_m_w, conv_m_b, mh_norm_g, w_out, norm2_g, w_up, conv_f_w, conv_f_b, w_down, final_g):
    assert w_ada.shape[0] == 1, "single-layer trunk"
    assert x_prompt.shape[1] == x_sample.shape[1]
    prep = _prepare(x_prompt.shape[1], norm1_g[0], w_in[0], b_gate[0], q_norm_g[0], kv_norm_g[0], w_uq[0],
                    w_ukv[0], conv_m_w[0], conv_m_b[0], mh_norm_g[0], w_out[0], norm2_g[0], w_up[0],
                    conv_f_w[0], conv_f_b[0], w_down[0], final_g)
    y_prompt = _trunk(x_prompt, c_prompt, w_ada[0], b_ada[0], prep)
    y_sample = _trunk(x_sample, c_sample, w_ada[0], b_ada[0], prep)
    return (y_prompt, y_sample)
```

```python
import functools
import math

import jax
import jax.numpy as jnp
from jax import lax
from jax.experimental import pallas as pl
from jax.experimental.pallas import tpu as pltpu

F32 = jnp.float32
BF16 = jnp.bfloat16

D_MODEL = 1024
N_HEADS = 8
D_NOPE = 64
D_ROPE = 32
D_V = 64
Q_RANK = 384
KV_RANK = 256
M_HEADS = 4
M_DIM = 128
D_MLSTM = M_HEADS * M_DIM
CHUNK = 128
M_CPS = 8
D_FF = 2816
ROPE_THETA = 10000.0
EPS = 1e-6
LANES = 128
SUBLANES = 8
FF_CHUNK = 256
FFN_ROWS = 256
ST_ROWS = M_DIM + 16
ATT_HPS = 2
SCORE_LIMIT = 40.0
VALUE_LIMIT = 2.0 ** 60
PV_ROWS = 80
VMEM_LIMIT = 56 * 1024 * 1024

_A_W = Q_RANK + KV_RANK + 2 * LANES
_QK0 = _A_W
_O0 = _QK0 + 2 * D_MLSTM
_G0 = _O0 + D_MLSTM
_N_PREP = _G0 + LANES


def _dot(a, b):
    return jnp.dot(a, b, preferred_element_type=F32)


def _dot_nt(a, b):
    return lax.dot_general(a, b, (((1,), (1,)), ((), ())), preferred_element_type=F32)


def _dot_tn(a, b):
    return lax.dot_general(a, b, (((0,), (0,)), ((), ())), preferred_element_type=F32)


def _rmsnorm(x, g):
    return x * lax.rsqrt(jnp.mean(x * x, axis=-1, keepdims=True) + EPS) * g


def _log_sigmoid(x):
    return jnp.minimum(x, 0.0) - jnp.log1p(jnp.exp(-jnp.abs(x)))


def _params(sem):
    return pltpu.CompilerParams(dimension_semantics=sem, vmem_limit_bytes=VMEM_LIMIT)


def _const_spec(shape):
    nd = len(shape)
    return pl.BlockSpec(shape, lambda *_: (0,) * nd, pipeline_mode=pl.Buffered(1))


def _halo_specs(tm, S, width):
    per = tm // SUBLANES
    nsub = S // SUBLANES
    return [
        pl.BlockSpec((1, SUBLANES, width), lambda b, i: (b, jnp.maximum(i * per - 1, 0), 0)),
        pl.BlockSpec((1, SUBLANES, width), lambda b, i: (b, jnp.minimum((i + 1) * per, nsub - 1), 0)),
    ]


def _conv3(u, cw_ref, cb_ref, cols, tm):
    rows = tm + 2 * SUBLANES
    full = (cw_ref[0:1, cols] * pltpu.roll(u, 1, axis=0) + cw_ref[1:2, cols] * u
            + cw_ref[2:3, cols] * pltpu.roll(u, rows - 1, axis=0) + cb_ref[:, cols])
    return full[SUBLANES:SUBLANES + tm]


def _ada_kernel(c_ref, w_ref, b_ref, o_ref):
    c = c_ref[...]
    a = (c * jax.nn.sigmoid(c)).astype(BF16)
    o_ref[...] = _dot(a, w_ref[...].astype(BF16)) + b_ref[...]


def _ada_mod(c, w_ada, b_ada):
    B = c.shape[0]
    cp = jnp.zeros((SUBLANES, D_MODEL), F32).at[:B].set(c)
    n = 6 * D_MODEL
    out = pl.pallas_call(
        _ada_kernel,
        out_shape=jax.ShapeDtypeStruct((SUBLANES, n), F32),
        grid=(6,),
        in_specs=[
            pl.BlockSpec((SUBLANES, D_MODEL), lambda j: (0, 0)),
            pl.BlockSpec((D_MODEL, D_MODEL), lambda j: (0, j)),
            pl.BlockSpec((1, D_MODEL), lambda j: (0, j)),
        ],
        out_specs=pl.BlockSpec((SUBLANES, D_MODEL), lambda j: (0, j)),
        compiler_params=_params(("parallel",)),
        name="ada_mod",
    )(cp, w_ada, b_ada.reshape(1, n))
    return out[:B].reshape(B, 1, n)


def _in_proj_kernel(x_ref, xp_ref, xn_ref, mod_ref, n1_ref, w_ref, bg_ref, qn_ref, kvn_ref, wq_ref, wk_ref,
                    wvt_ref, cos_ref, sin_ref, one_ref, cw_ref, cb_ref, wvm_ref,
                    q_ref, k_ref, vt_ref, qk_ref, vm_ref, om_ref, g_ref, st_ref):
    i = pl.program_id(1)
    nt = pl.num_programs(1)
    tm = x_ref.shape[1]
    sh1 = mod_ref[0, :, 0:D_MODEL]
    sc1 = mod_ref[0, :, D_MODEL:2 * D_MODEL]

    def norm_mod(r):
        return _rmsnorm(r, n1_ref[...]) * (1.0 + sc1) + sh1

    hf = norm_mod(x_ref[0])
    h = hf.astype(BF16)
    hp = jnp.where(i > 0, norm_mod(xp_ref[0]), 0.0)
    hn = jnp.where(i < nt - 1, norm_mod(xn_ref[0]), 0.0)
    hext = jnp.concatenate([hp, hf, hn], axis=0).astype(BF16)
    pa = _dot(h, w_ref[:, 0:_A_W])
    vm_ref[0] = _dot_nt(wvm_ref[...], h).astype(BF16)
    om_ref[0] = _dot(h, w_ref[:, _O0:_G0])
    g = _dot(h, w_ref[:, _G0:_N_PREP]) + bg_ref[...]
    lane = lax.broadcasted_iota(jnp.int32, g.shape, 1)
    is_forget = (lane < 4 * M_HEADS) & ((lane // M_HEADS) % 2 == 1)
    g_ref[0] = jnp.where(is_forget, _log_sigmoid(g), g)

    conv = _conv3(_dot(hext, w_ref[:, _QK0:_O0]), cw_ref, cb_ref, slice(None), tm)
    qk = conv * jax.nn.sigmoid(conv)
    qk_ref[0, :, 0:D_MLSTM] = (qk[:, 0:D_MLSTM] * (M_DIM ** -0.5)).astype(BF16)
    qk_ref[0, :, D_MLSTM:] = qk[:, D_MLSTM:].astype(BF16)

    cos = cos_ref[...]
    sin = sin_ref[...]
    cos8 = jnp.concatenate([cos] * N_HEADS, axis=1)
    sin8 = jnp.concatenate([sin] * N_HEADS, axis=1)
    hw = N_HEADS * LANES

    cqn = _rmsnorm(pa[:, 0:Q_RANK], qn_ref[...]).astype(BF16)
    qq = _dot(cqn, wq_ref[...])
    q = (qq[:, 0:hw] * cos8 + qq[:, hw:2 * hw] * sin8).astype(BF16)

    ckvn = _rmsnorm(pa[:, Q_RANK:Q_RANK + KV_RANK], kvn_ref[...]).astype(BF16)
    kk = _dot(ckvn, wk_ref[...])
    kr0 = Q_RANK + KV_RANK
    kr = pa[:, kr0:kr0 + LANES] * cos + pa[:, kr0 + LANES:kr0 + 2 * LANES] * sin
    k = (kk + jnp.concatenate([kr] * N_HEADS, axis=1)).astype(BF16)
    vt = (_dot_nt(wvt_ref[...], ckvn) + one_ref[...]).astype(BF16)
    for hd in range(N_HEADS):
        sl = slice(hd * LANES, (hd + 1) * LANES)
        q_ref[0, hd] = q[:, sl]
        k_ref[0, hd] = k[:, sl]
        vt_ref[0, hd, 0] = vt[sl, :]
        kh = k[:, sl].astype(F32)
        k2 = jnp.max(jnp.sum(kh * kh, axis=1, keepdims=True), axis=0, keepdims=True)
        va = jnp.max(jnp.max(jnp.abs(vt[sl, :].astype(F32)), axis=1, keepdims=True), axis=0, keepdims=True)
        st_ref[0, 0, hd:hd + 1, :] = jnp.broadcast_to(k2, (1, LANES))
        st_ref[0, 0, N_HEADS + hd:N_HEADS + hd + 1, :] = jnp.broadcast_to(va, (1, LANES))


def _in_proj(x, mod, prep, tm):
    B, S, _ = x.shape
    grid = (B, S // tm)
    row = lambda b, i: (b, i, 0)
    hrow = lambda b, i: (b, 0, i, 0)
    head_shape = jax.ShapeDtypeStruct((B, N_HEADS, S, LANES), BF16)
    hw = N_HEADS * LANES
    return pl.pallas_call(
        _in_proj_kernel,
        out_shape=(
            head_shape, head_shape,
            jax.ShapeDtypeStruct((B, N_HEADS, S // tm, LANES, tm), BF16),
            jax.ShapeDtypeStruct((B, S, 2 * D_MLSTM), BF16),
            jax.ShapeDtypeStruct((B, D_MLSTM, S), BF16),
            jax.ShapeDtypeStruct((B, S, D_MLSTM), F32),
            jax.ShapeDtypeStruct((B, S, LANES), F32),
            jax.ShapeDtypeStruct((B, S // tm, 2 * N_HEADS, LANES), F32),
        ),
        grid=grid,
        in_specs=[pl.BlockSpec((1, tm, D_MODEL), row)] + _halo_specs(tm, S, D_MODEL) + [
            pl.BlockSpec((1, 1, 6 * D_MODEL), lambda b, i: (b, 0, 0)),
            _const_spec((1, D_MODEL)),
            _const_spec((D_MODEL, _N_PREP)),
            _const_spec((1, LANES)),
            _const_spec((1, Q_RANK)),
            _const_spec((1, KV_RANK)),
            _const_spec((Q_RANK, 2 * hw)),
            _const_spec((KV_RANK, hw)),
            _const_spec((hw, KV_RANK)),
            pl.BlockSpec((tm, LANES), lambda b, i: (i, 0)),
            pl.BlockSpec((tm, LANES), lambda b, i: (i, 0)),
            _const_spec((hw, 1)),
            _const_spec((3, 2 * D_MLSTM)),
            _const_spec((1, 2 * D_MLSTM)),
            _const_spec((D_MLSTM, D_MODEL)),
        ],
        out_specs=(
            pl.BlockSpec((1, N_HEADS, tm, LANES), hrow),
            pl.BlockSpec((1, N_HEADS, tm, LANES), hrow),
            pl.BlockSpec((1, N_HEADS, 1, LANES, tm), lambda b, i: (b, 0, i, 0, 0)),
            pl.BlockSpec((1, tm, 2 * D_MLSTM), row),
            pl.BlockSpec((1, D_MLSTM, tm), lambda b, i: (b, 0, i)),
            pl.BlockSpec((1, tm, D_MLSTM), row),
            pl.BlockSpec((1, tm, LANES), row),
            pl.BlockSpec((1, 1, 2 * N_HEADS, LANES), lambda b, i: (b, i, 0, 0)),
        ),
        compiler_params=_params(("parallel", "parallel")),
        name="in_proj",
    )(x, x, x, mod, prep["norm1_g"], prep["w_in"], prep["b_gate"], prep["q_norm_g"], prep["kv_norm_g"],
      prep["w_q"], prep["w_k"], prep["w_vt"], prep["cos"], prep["sin"], prep["v_one"],
      prep["conv_m_w"], prep["conv_m_b"], prep["w_vm_t"])


def _attn_kernel(thr_ref, q_ref, k_ref, vt_ref, o_ref, s_scr, *, sub):
    b = pl.program_id(0)
    hg = pl.program_id(1)
    hps = q_ref.shape[1]
    tq = q_ref.shape[2]
    ck = vt_ref.shape[4]
    tk = sub * ck
    n = k_ref.shape[2] // tk

    qtf = [q_ref[0, hh].astype(F32).T for hh in range(hps)]
    qts = [x.astype(BF16) for x in qtf]
    safe = None
    for hh in range(hps):
        q2 = jnp.max(jnp.sum(qtf[hh] * qtf[hh], axis=0, keepdims=True))
        ok = q2 <= thr_ref[b, hg * hps + hh]
        safe = ok if safe is None else jnp.logical_and(safe, ok)

    def value_dot(hh, c, p, acc=None):
        for u in range(sub):
            pv = _dot(vt_ref[0, hh, c * sub + u, 0:PV_ROWS, :], p[u * ck:(u + 1) * ck])
            acc = pv if acc is None else pv + acc
        return acc

    def store(hh, acc):
        out = acc[0:D_V] / acc[D_V:D_V + 1]
        o_ref[0, hh] = jnp.concatenate([out, jnp.zeros((LANES - D_V, tq), F32)], axis=0).T.astype(BF16)

    @pl.when(safe)
    def _():
        def scores(hh, c):
            return _dot(k_ref[0, hh, c * tk:(c + 1) * tk, :], qts[hh])

        acc = [None] * hps
        s_cur = [scores(hh, 0) for hh in range(hps)]
        for c in range(n):
            for hh in range(hps):
                s = s_cur[hh]
                if c + 1 < n:
                    s_cur[hh] = scores(hh, c + 1)
                acc[hh] = value_dot(hh, c, jnp.exp2(s).astype(BF16), acc[hh])
        for hh in range(hps):
            store(hh, acc[hh])

    @pl.when(jnp.logical_not(safe))
    def _():
        for hh in range(hps):
            def scores(c, slot, hh=hh):
                off = pl.multiple_of(c * tk, tk)
                s = _dot(k_ref[0, hh, pl.ds(off, tk), :], qts[hh])
                s_scr[slot] = s
                return jnp.max(s, axis=0, keepdims=True)

            def update(c, slot, mx, m, acc, hh=hh):
                m_new = jnp.maximum(m, mx)
                p = jnp.exp2(s_scr[slot] - m_new).astype(BF16)
                return m_new, jnp.exp2(m - m_new) * acc + value_dot(hh, c, p)

            def body(jj, carry):
                m, acc, mx_a = carry
                c = 2 * jj
                mx_b = scores(c + 1, 1)
                m, acc = update(c, 0, mx_a, m, acc)
                mx_a = scores(jnp.minimum(c + 2, n - 1), 0)
                m, acc = update(c + 1, 1, mx_b, m, acc)
                return m, acc, mx_a

            init = (jnp.full((1, tq), -jnp.inf, F32), jnp.zeros((PV_ROWS, tq), F32), scores(0, 0))
            _, acc, _ = lax.fori_loop(0, n // 2, body, init)
            store(hh, acc)


def _attention(q, k, vt, stats, tq, sub):
    B, H, S, _ = q.shape
    ck = vt.shape[4]
    tk = sub * ck
    assert S % (2 * tk) == 0 and S % tq == 0
    kmax2 = jnp.max(stats[:, :, 0:N_HEADS, 0], axis=1)
    vmax = jnp.max(stats[:, :, N_HEADS:, 0], axis=1)
    thr = jnp.where(vmax <= VALUE_LIMIT, SCORE_LIMIT ** 2 / jnp.maximum(kmax2, 1e-30), -1.0)
    grid_spec = pltpu.PrefetchScalarGridSpec(
        num_scalar_prefetch=1,
        grid=(B, H // ATT_HPS, S // tq),
        in_specs=[
            pl.BlockSpec((1, ATT_HPS, tq, LANES), lambda b, h, i, thr: (b, h, i, 0)),
            pl.BlockSpec((1, ATT_HPS, S, LANES), lambda b, h, i, thr: (b, h, 0, 0)),
            pl.BlockSpec((1, ATT_HPS, S // ck, LANES, ck), lambda b, h, i, thr: (b, h, 0, 0, 0)),
        ],
        out_specs=pl.BlockSpec((1, ATT_HPS, tq, LANES), lambda b, h, i, thr: (b, h, i, 0)),
        scratch_shapes=[pltpu.VMEM((2, tk, tq), F32)],
    )
    return pl.pallas_call(
        functools.partial(_attn_kernel, sub=sub),
        out_shape=jax.ShapeDtypeStruct((B, H, S, LANES), BF16),
        grid_spec=grid_spec,
        compiler_params=_params(("parallel", "parallel", "parallel")),
        name="attention",
    )(thr, q, k, vt)


def _split3(x):
    hi = x.astype(BF16)
    r1 = x - hi.astype(F32)
    mid = r1.astype(BF16)
    lo = (r1 - mid.astype(F32)).astype(BF16)
    return hi, mid, lo


def _mlstm_chunks(directions):
    L = CHUNK
    si = lax.broadcasted_iota(jnp.int32, (L, L), 0)
    ti = lax.broadcasted_iota(jnp.int32, (L, L), 1)
    one_rows = jnp.where(lax.broadcasted_iota(jnp.int32, (ST_ROWS - M_DIM, L), 0) == 0, 1.0, 0.0).astype(BF16)

    units = [d[:6] + (sub, d[7]) for d in directions for sub in d[6]]
    gates = []
    for (_, _, g_ref, _, _, _, sub, reverse) in units:
        g = g_ref[0, sub * L:(sub + 1) * L, :]
        gt = g.T
        vis = (si >= ti) if reverse else (si <= ti)
        vis_bf = jnp.where(vis, 1.0, 0.0).astype(BF16)
        vis_t_bf = jnp.where((ti >= si) if reverse else (ti <= si), 1.0, 0.0).astype(BF16)
        b_cols = sum(_dot(vis_t_bf, piece) for piece in _split3(g))
        b_rows = sum(_dot(piece, vis_bf) for piece in _split3(gt))
        gates.append((g, gt, vis, b_cols, b_rows))

    chains = [(ui, hd) for ui in range(len(units)) for hd in range(M_HEADS)]
    n = range(len(chains))
    qb, kb, vt_ext = [], [], []
    for ui, hd in chains:
        qk_ref, vt_ref, _, _, _, _, sub, _ = units[ui]
        rs = slice(sub * L, (sub + 1) * L)
        sl = slice(hd * M_DIM, (hd + 1) * M_DIM)
        qb.append(qk_ref[0, rs, sl])
        kb.append(qk_ref[0, rs, D_MLSTM + hd * M_DIM:D_MLSTM + (hd + 1) * M_DIM])
        vt_ext.append(jnp.concatenate([vt_ref[0, sl, rs], one_rows], axis=0))

    qk_t = [_dot_nt(kb[c], qb[c]) for c in n]

    b_r, dmat, a, b_last = [], [], [], []
    for ui, hd in chains:
        g, gt, vis, b_cols, b_rows = gates[ui]
        reverse = units[ui][7]
        i_off = 2 * M_HEADS if reverse else 0
        f_off = i_off + M_HEADS
        last = 0 if reverse else L - 1
        br = b_rows[f_off + hd:f_off + hd + 1, :]
        u_c = g[:, i_off + hd:i_off + hd + 1] - b_cols[:, f_off + hd:f_off + hd + 1]
        b_r.append(br)
        dmat.append(jnp.where(vis, br + u_c, -jnp.inf))
        b_last.append(br[:, last:last + 1])
        a.append(br[:, last:last + 1] - br + gt[i_off + hd:i_off + hd + 1, :])

    order = {}
    for c, (ui, hd) in enumerate(chains):
        order.setdefault((id(units[ui][4]), hd), []).append(c)
    m_prev, m_new = [None] * len(chains), [None] * len(chains)
    for cs in order.values():
        m = units[chains[cs[0]][0]][5][chains[cs[0]][1]]
        for c in cs:
            m_prev[c] = m
            m = jnp.maximum(b_last[c] + m, jnp.max(a[c], axis=1, keepdims=True))
            m_new[c] = m

    s_bf, w_state, m_t, vw, decay = [], [], [], [], []
    for c in n:
        inter = b_r[c] + m_prev[c]
        m_t.append(jnp.maximum(inter, jnp.max(dmat[c], axis=0, keepdims=True)))
        w_state.append(jnp.exp(inter - m_t[c]))
        s_bf.append((qk_t[c] * jnp.exp(dmat[c] - m_t[c])).astype(BF16))
        w_s = jnp.exp(a[c] - m_new[c])
        decay.append(jnp.exp(b_last[c] + m_prev[c] - m_new[c]))
        vw.append((vt_ext[c].astype(F32) * w_s).astype(BF16))

    upd = [_dot(vw[c], kb[c]) for c in n]

    c_prev = [None] * len(chains)
    for cs in order.values():
        ui, hd = chains[cs[0]]
        c_scr, m_scr = units[ui][4], units[ui][5]
        state = c_scr[hd]
        for c in cs:
            c_prev[c] = state
            state = decay[c] * state + upd[c]
        c_scr[hd] = state
        m_scr[hd] = m_new[cs[-1]]

    cq = [_dot_nt(c_prev[c].astype(BF16), qb[c]) for c in n]
    tot = [_dot(vt_ext[c], s_bf[c]) + w_state[c] * cq[c] for c in n]
    for c, (ui, hd) in enumerate(chains):
        h_ref, sub = units[ui][3], units[ui][6]
        den = tot[c][M_DIM:M_DIM + 1, :]
        h_t = tot[c][0:M_DIM] / jnp.maximum(jnp.abs(den), jnp.exp(-m_t[c]))
        h_ref[0, sub * L:(sub + 1) * L, hd * M_DIM:(hd + 1) * M_DIM] = h_t.T


def _mlstm_kernel(qkf_ref, vf_ref, gf_ref, qkb_ref, vb_ref, gb_ref,
                  hf_ref, hb_ref, cf_scr, mf_scr, cb_scr, mb_scr):
    @pl.when(pl.program_id(1) == 0)
    def _():
        for r in (cf_scr, mf_scr, cb_scr, mb_scr):
            r[...] = jnp.zeros_like(r)

    subs = list(range(M_CPS))
    _mlstm_chunks([(qkf_ref, vf_ref, gf_ref, hf_ref, cf_scr, mf_scr, subs, False),
                   (qkb_ref, vb_ref, gb_ref, hb_ref, cb_scr, mb_scr, subs[::-1], True)])


def _mlstm(qk_act, v_m, gates):
    B, S, _ = qk_act.shape
    rows = M_CPS * CHUNK
    nc = S // rows

    def specs(cidx):
        main = lambda b, i: (b, cidx(i), 0)
        return [
            pl.BlockSpec((1, rows, 2 * D_MLSTM), main),
            pl.BlockSpec((1, D_MLSTM, rows), lambda b, i: (b, 0, cidx(i))),
            pl.BlockSpec((1, rows, LANES), main),
        ]

    fwd = lambda i: i
    bwd = lambda i: nc - 1 - i
    state = [pltpu.VMEM((M_HEADS, ST_ROWS, M_DIM), F32), pltpu.VMEM((M_HEADS, 1, 1), F32)]
    h_shape = jax.ShapeDtypeStruct((B, S, D_MLSTM), F32)
    return pl.pallas_call(
        _mlstm_kernel,
        out_shape=(h_shape, h_shape),
        grid=(B, nc),
        in_specs=specs(fwd) + specs(bwd),
        out_specs=(
            pl.BlockSpec((1, rows, D_MLSTM), lambda b, i: (b, i, 0)),
            pl.BlockSpec((1, rows, D_MLSTM), lambda b, i: (b, nc - 1 - i, 0)),
        ),
        scratch_shapes=state + state,
        compiler_params=_params(("parallel", "arbitrary")),
        name="mlstm",
    )(qk_act, v_m, gates, qk_act, v_m, gates)


def _out_proj_kernel(x_ref, mod_ref, a_ref, hf_ref, hb_ref, om_ref, mg_ref, wa_ref, wm_ref, o_ref):
    g1 = mod_ref[0, :, 2 * D_MODEL:3 * D_MODEL]
    attn = jnp.concatenate([a_ref[0, hd] for hd in range(N_HEADS)], axis=1)
    y = _dot(attn, wa_ref[...])
    h = hf_ref[0] + hb_ref[0]
    om = om_ref[0]
    parts = []
    for hd in range(M_HEADS):
        sl = slice(hd * M_DIM, (hd + 1) * M_DIM)
        hh = h[:, sl]
        mu = jnp.mean(hh, axis=1, keepdims=True)
        d = hh - mu
        var = jnp.mean(d * d, axis=1, keepdims=True)
        parts.append(d * lax.rsqrt(var + EPS) * mg_ref[:, sl] * jax.nn.sigmoid(om[:, sl]))
    mem = jnp.concatenate(parts, axis=1).astype(BF16)
    y = y + _dot(mem, wm_ref[...])
    o_ref[0] = x_ref[0] + g1 * y


def _out_proj(x, mod, attn, h_f, h_b, o_m, prep, tm):
    B, S, _ = x.shape
    row = lambda b, i: (b, i, 0)
    return pl.pallas_call(
        _out_proj_kernel,
        out_shape=jax.ShapeDtypeStruct((B, S, D_MODEL), F32),
        grid=(B, S // tm),
        in_specs=[
            pl.BlockSpec((1, tm, D_MODEL), row),
            pl.BlockSpec((1, 1, 6 * D_MODEL), lambda b, i: (b, 0, 0)),
            pl.BlockSpec((1, N_HEADS, tm, LANES), lambda b, i: (b, 0, i, 0)),
            pl.BlockSpec((1, tm, D_MLSTM), row),
            pl.BlockSpec((1, tm, D_MLSTM), row),
            pl.BlockSpec((1, tm, D_MLSTM), row),
            _const_spec((1, D_MLSTM)),
            _const_spec((N_HEADS * LANES, D_MODEL)),
            _const_spec((D_MLSTM, D_MODEL)),
        ],
        out_specs=pl.BlockSpec((1, tm, D_MODEL), row),
        compiler_params=_params(("parallel", "parallel")),
        name="out_proj",
    )(x, mod, attn, h_f, h_b, o_m, prep["mh_norm_g"], prep["w_out_attn"], prep["w_out_mem"])


def _ffn_kernel(x_ref, xp_ref, xn_ref, mod_ref, n2_ref, wu_ref, cw_ref, cb_ref, wd_ref, fg_ref, o_ref):
    i = pl.program_id(1)
    nt = pl.num_programs(1)
    tm = x_ref.shape[1]
    sh2 = mod_ref[0, :, 3 * D_MODEL:4 * D_MODEL]
    sc2 = mod_ref[0, :, 4 * D_MODEL:5 * D_MODEL]
    g2 = mod_ref[0, :, 5 * D_MODEL:6 * D_MODEL]

    def norm_mod(r):
        return _rmsnorm(r, n2_ref[...]) * (1.0 + sc2) + sh2

    x = x_ref[0]
    hp = jnp.where(i > 0, norm_mod(xp_ref[0]), 0.0)
    hn = jnp.where(i < nt - 1, norm_mod(xn_ref[0]), 0.0)
    hext = jnp.concatenate([hp, norm_mod(x), hn], axis=0).astype(BF16)

    n_chunks = D_FF // FF_CHUNK
    ca = [slice(j * FF_CHUNK, (j + 1) * FF_CHUNK) for j in range(n_chunks)]
    cg = [slice(D_FF + j * FF_CHUNK, D_FF + (j + 1) * FF_CHUNK) for j in range(n_chunks)]

    def up(j):
        return _dot(hext, wu_ref[:, ca[j]]), _dot(hext, wu_ref[:, cg[j]])

    acc = jnp.zeros((tm, D_MODEL), F32)
    nxt = up(0)
    for j in range(n_chunks):
        ua, ug = nxt
        if j + 1 < n_chunks:
            nxt = up(j + 1)
        a = _conv3(ua, cw_ref, cb_ref, ca[j], tm)
        g = _conv3(ug, cw_ref, cb_ref, cg[j], tm)
        act = (g * jax.nn.sigmoid(g) * a).astype(BF16)
        acc = acc + _dot(act, wd_ref[ca[j], :])
    o_ref[0] = _rmsnorm(x + g2 * acc, fg_ref[...])


def _conv_ffn(x, mod, prep, tm):
    B, S, _ = x.shape
    row = lambda b, i: (b, i, 0)
    return pl.pallas_call(
        _ffn_kernel,
        out_shape=jax.ShapeDtypeStruct((B, S, D_MODEL), F32),
        grid=(B, S // tm),
        in_specs=[pl.BlockSpec((1, tm, D_MODEL), row)] + _halo_specs(tm, S, D_MODEL) + [
            pl.BlockSpec((1, 1, 6 * D_MODEL), lambda b, i: (b, 0, 0)),
            _const_spec((1, D_MODEL)),
            _const_spec((D_MODEL, 2 * D_FF)),
            _const_spec((3, 2 * D_FF)),
            _const_spec((1, 2 * D_FF)),
            _const_spec((D_FF, D_MODEL)),
            _const_spec((1, D_MODEL)),
        ],
        out_specs=pl.BlockSpec((1, tm, D_MODEL), row),
        compiler_params=_params(("parallel", "parallel")),
        name="conv_ffn",
    )(x, x, x, mod, prep["norm2_g"], prep["w_up"], prep["conv_f_w"], prep["conv_f_b"],
      prep["w_down"], prep["final_g"])


def _head_groups(w, n_heads, width, pieces):
    rows = w.shape[0]
    wh = w.reshape(rows, n_heads, width)
    out = jnp.zeros((rows, n_heads, LANES), F32)
    for src, n, dst, sign in pieces:
        out = out.at[:, :, dst:dst + n].set(sign * wh[:, :, src:src + n])
    return out.reshape(rows, n_heads * LANES)


def _prepare(S, norm1_g, w_in, b_gate, q_norm_g, kv_norm_g, w_uq, w_ukv, conv_m_w, conv_m_b,
             mh_norm_g, w_out, norm2_g, w_up, conv_f_w, conv_f_b, w_down, final_g):
    half = D_ROPE // 2
    r1 = D_NOPE
    r2 = D_NOPE + half
    c1, c2 = Q_RANK + KV_RANK, Q_RANK + KV_RANK + D_ROPE
    c3 = c2 + 2 * D_MLSTM
    c4 = c3 + D_MLSTM
    c5 = c4 + D_MLSTM

    kr = w_in[:, c1:c2]
    zeros = lambda n: jnp.zeros((D_MODEL, n), F32)
    kr_main = jnp.concatenate([zeros(r1), kr[:, :half], kr[:, half:], zeros(LANES - r2 - half)], axis=1)
    kr_swap = jnp.concatenate([zeros(r1), -kr[:, half:], kr[:, :half], zeros(LANES - r2 - half)], axis=1)
    gates = jnp.concatenate([w_in[:, c5:], zeros(LANES - 4 * M_HEADS)], axis=1)
    w_prep = jnp.concatenate([w_in[:, :c1], kr_main, kr_swap, w_in[:, c2:c3], w_in[:, c4:c5], gates], axis=1)

    qs = (D_NOPE + D_ROPE) ** -0.5 * math.log2(math.e)
    dq = D_NOPE + D_ROPE
    wq_main = _head_groups(w_uq, N_HEADS, dq, [(0, D_NOPE, 0, 1.0), (D_NOPE, half, r1, 1.0),
                                                (D_NOPE + half, half, r2, 1.0)])
    wq_swap = _head_groups(w_uq, N_HEADS, dq, [(D_NOPE + half, half, r1, -1.0), (D_NOPE, half, r2, 1.0)])
    w_q = jnp.concatenate([wq_main, wq_swap], axis=1) * qs

    dkv = D_NOPE + D_V
    wk = _head_groups(w_ukv, N_HEADS, dkv, [(0, D_NOPE, 0, 1.0)])
    wv = _head_groups(w_ukv, N_HEADS, dkv, [(D_NOPE, D_V, 0, 1.0)])

    inv = 1.0 / (ROPE_THETA ** (jnp.arange(half, dtype=F32) * (2.0 / D_ROPE)))
    ang = jnp.arange(S, dtype=jnp.int32).astype(F32)[:, None] * inv[None, :]
    ones = jnp.ones((S, r1), F32)
    pad = jnp.zeros((S, LANES - r2 - half), F32)
    cos = jnp.concatenate([ones, jnp.cos(ang), jnp.cos(ang), pad], axis=1)
    sin = jnp.concatenate([0.0 * ones, jnp.sin(ang), jnp.sin(ang), pad], axis=1)

    v_one = jnp.zeros((N_HEADS, LANES), F32).at[:, D_V].set(1.0).reshape(N_HEADS * LANES, 1)
    b_g = jnp.concatenate([b_gate, jnp.zeros((LANES - 4 * M_HEADS,), F32)]).reshape(1, LANES)

    d_attn = N_HEADS * D_V
    w_oa = jnp.zeros((N_HEADS, LANES, D_MODEL), F32).at[:, :D_V, :].set(
        w_out[:d_attn].reshape(N_HEADS, D_V, D_MODEL)).reshape(N_HEADS * LANES, D_MODEL)

    return dict(
        norm1_g=norm1_g.reshape(1, -1), w_in=w_prep.astype(BF16), w_vm_t=w_in[:, c3:c4].T.astype(BF16), b_gate=b_g,
        q_norm_g=q_norm_g.reshape(1, -1), kv_norm_g=kv_norm_g.reshape(1, -1),
        w_q=w_q.astype(BF16), w_k=wk.astype(BF16), w_vt=wv.T.astype(BF16), cos=cos, sin=sin, v_one=v_one,
        conv_m_w=conv_m_w, conv_m_b=conv_m_b.reshape(1, -1), mh_norm_g=mh_norm_g.reshape(1, -1),
        w_out_attn=w_oa.astype(BF16), w_out_mem=w_out[d_attn:].astype(BF16),
        norm2_g=norm2_g.reshape(1, -1), w_up=w_up.astype(BF16), conv_f_w=conv_f_w,
        conv_f_b=conv_f_b.reshape(1, -1), w_down=w_down.astype(BF16), final_g=final_g.reshape(1, -1),
    )


def _tiles(S):
    tm = min(512, S // 2)
    sub = 2 if S % (4 * tm) == 0 else 1
    return tm, min(512, S), sub


def _trunk(x, c, w_ada, b_ada, prep):
    S = x.shape[1]
    tm, tq, sub = _tiles(S)
    mod = _ada_mod(c, w_ada, b_ada)
    q, k, vt, qk_act, v_m, o_m, gates, stats = _in_proj(x, mod, prep, tm)
    attn = _attention(q, k, vt, stats, tq, sub)
    h_f, h_b = _mlstm(qk_act, v_m, gates)
    x1 = _out_proj(x, mod, attn, h_f, h_b, o_m, prep, tm)
    return _conv_ffn(x1, mod, prep, min(FFN_ROWS, tm))


def kernel(x_prompt, x_sample, c_prompt, c_sample, norm1_g, w_ada, b_ada, w_in, b_gate, q_norm_g, kv_norm_g, w_uq, w_ukv, conv_m_w, conv_m_b, mh_norm_g, w_out, norm2_g, w_up, conv_f_w, conv_f_b, w_down, final_g):
    assert w_ada.shape[0] == 1, "single-layer trunk"
    assert x_prompt.shape[1] == x_sample.shape[1]
    prep = _prepare(x_prompt.shape[1], norm1_g[0], w_in[0], b_gate[0], q_norm_g[0], kv_norm_g[0], w_uq[0],
                    w_ukv[0], conv_m_w[0], conv_m_b[0], mh_norm_g[0], w_out[0], norm2_g[0], w_up[0],
                    conv_f_w[0], conv_f_b[0], w_down[0], final_g)
    y_prompt = _trunk(x_prompt, c_prompt, w_ada[0], b_ada[0], prep)
    y_sample = _trunk(x_sample, c_sample, w_ada[0], b_ada[0], prep)
    return (y_prompt, y_sample)
```

```python
import functools
import math

import jax
import jax.numpy as jnp
from jax import lax
from jax.experimental import pallas as pl
from jax.experimental.pallas import tpu as pltpu

F32 = jnp.float32
BF16 = jnp.bfloat16

D_MODEL = 1024
N_HEADS = 8
D_NOPE = 64
D_ROPE = 32
D_V = 64
Q_RANK = 384
KV_RANK = 256
M_HEADS = 4
M_DIM = 128
D_MLSTM = M_HEADS * M_DIM
CHUNK = 128
M_CPS = 8
D_FF = 2816
ROPE_THETA = 10000.0
EPS = 1e-6
LANES = 128
SUBLANES = 8
FF_CHUNK = 256
FF_AHEAD = 2
FFN_ROWS = 256
ST_ROWS = M_DIM + 16
ATT_HPS = 2
ATT_AHEAD = 1
SCORE_LIMIT = 40.0
VALUE_LIMIT = 2.0 ** 60
PV_ROWS = 80
VMEM_LIMIT = 56 * 1024 * 1024

_A_W = Q_RANK + KV_RANK + 2 * LANES
_QK0 = _A_W
_O0 = _QK0 + 2 * D_MLSTM
_G0 = _O0 + D_MLSTM
_N_PREP = _G0 + LANES


def _dot(a, b):
    return jnp.dot(a, b, preferred_element_type=F32)


def _dot_nt(a, b):
    return lax.dot_general(a, b, (((1,), (1,)), ((), ())), preferred_element_type=F32)


def _dot_tn(a, b):
    return lax.dot_general(a, b, (((0,), (0,)), ((), ())), preferred_element_type=F32)


def _rmsnorm(x, g):
    return x * lax.rsqrt(jnp.mean(x * x, axis=-1, keepdims=True) + EPS) * g


def _log_sigmoid(x):
    return jnp.minimum(x, 0.0) - jnp.log1p(jnp.exp(-jnp.abs(x)))


def _params(sem):
    return pltpu.CompilerParams(dimension_semantics=sem, vmem_limit_bytes=VMEM_LIMIT)


def _const_spec(shape):
    nd = len(shape)
    return pl.BlockSpec(shape, lambda *_: (0,) * nd, pipeline_mode=pl.Buffered(1))


def _halo_specs(tm, S, width):
    per = tm // SUBLANES
    nsub = S // SUBLANES
    return [
        pl.BlockSpec((1, SUBLANES, width), lambda b, i: (b, jnp.maximum(i * per - 1, 0), 0)),
        pl.BlockSpec((1, SUBLANES, width), lambda b, i: (b, jnp.minimum((i + 1) * per, nsub - 1), 0)),
    ]


def _conv3(u, cw_ref, cb_ref, cols, tm):
    rows = tm + 2 * SUBLANES
    full = (cw_ref[0:1, cols] * pltpu.roll(u, 1, axis=0) + cw_ref[1:2, cols] * u
            + cw_ref[2:3, cols] * pltpu.roll(u, rows - 1, axis=0) + cb_ref[:, cols])
    return full[SUBLANES:SUBLANES + tm]


def _ada_kernel(c_ref, w_ref, b_ref, o_ref):
    c = c_ref[...]
    a = (c * jax.nn.sigmoid(c)).astype(BF16)
    o_ref[...] = _dot(a, w_ref[...].astype(BF16)) + b_ref[...]


def _ada_mod(c, w_ada, b_ada):
    B = c.shape[0]
    cp = jnp.zeros((SUBLANES, D_MODEL), F32).at[:B].set(c)
    n = 6 * D_MODEL
    out = pl.pallas_call(
        _ada_kernel,
        out_shape=jax.ShapeDtypeStruct((SUBLANES, n), F32),
        grid=(6,),
        in_specs=[
            pl.BlockSpec((SUBLANES, D_MODEL), lambda j: (0, 0)),
            pl.BlockSpec((D_MODEL, D_MODEL), lambda j: (0, j)),
            pl.BlockSpec((1, D_MODEL), lambda j: (0, j)),
        ],
        out_specs=pl.BlockSpec((SUBLANES, D_MODEL), lambda j: (0, j)),
        compiler_params=_params(("parallel",)),
        name="ada_mod",
    )(cp, w_ada, b_ada.reshape(1, n))
    return out[:B].reshape(B, 1, n)


def _in_proj_kernel(x_ref, xp_ref, xn_ref, mod_ref, n1_ref, w_ref, bg_ref, qn_ref, kvn_ref, wq_ref, wk_ref,
                    wvt_ref, cos_ref, sin_ref, one_ref, cw_ref, cb_ref, wvm_ref,
                    q_ref, k_ref, vt_ref, qk_ref, vm_ref, om_ref, g_ref, st_ref):
    i = pl.program_id(1)
    nt = pl.num_programs(1)
    tm = x_ref.shape[1]
    sh1 = mod_ref[0, :, 0:D_MODEL]
    sc1 = mod_ref[0, :, D_MODEL:2 * D_MODEL]

    def norm_mod(r):
        return _rmsnorm(r, n1_ref[...]) * (1.0 + sc1) + sh1

    hf = norm_mod(x_ref[0])
    h = hf.astype(BF16)
    hp = jnp.where(i > 0, norm_mod(xp_ref[0]), 0.0)
    hn = jnp.where(i < nt - 1, norm_mod(xn_ref[0]), 0.0)
    hext = jnp.concatenate([hp, hf, hn], axis=0).astype(BF16)
    pa = _dot(h, w_ref[:, 0:_A_W])
    vm_ref[0] = _dot_nt(wvm_ref[...], h).astype(BF16)
    om_ref[0] = _dot(h, w_ref[:, _O0:_G0])
    g = _dot(h, w_ref[:, _G0:_N_PREP]) + bg_ref[...]
    lane = lax.broadcasted_iota(jnp.int32, g.shape, 1)
    is_forget = (lane < 4 * M_HEADS) & ((lane // M_HEADS) % 2 == 1)
    g_ref[0] = jnp.where(is_forget, _log_sigmoid(g), g)

    conv = _conv3(_dot(hext, w_ref[:, _QK0:_O0]), cw_ref, cb_ref, slice(None), tm)
    qk = conv * jax.nn.sigmoid(conv)
    qk_ref[0, :, 0:D_MLSTM] = (qk[:, 0:D_MLSTM] * (M_DIM ** -0.5)).astype(BF16)
    qk_ref[0, :, D_MLSTM:] = qk[:, D_MLSTM:].astype(BF16)

    cos = cos_ref[...]
    sin = sin_ref[...]
    cos8 = jnp.concatenate([cos] * N_HEADS, axis=1)
    sin8 = jnp.concatenate([sin] * N_HEADS, axis=1)
    hw = N_HEADS * LANES

    cqn = _rmsnorm(pa[:, 0:Q_RANK], qn_ref[...]).astype(BF16)
    qq = _dot(cqn, wq_ref[...])
    q = (qq[:, 0:hw] * cos8 + qq[:, hw:2 * hw] * sin8).astype(BF16)

    ckvn = _rmsnorm(pa[:, Q_RANK:Q_RANK + KV_RANK], kvn_ref[...]).astype(BF16)
    kk = _dot(ckvn, wk_ref[...])
    kr0 = Q_RANK + KV_RANK
    kr = pa[:, kr0:kr0 + LANES] * cos + pa[:, kr0 + LANES:kr0 + 2 * LANES] * sin
    k = (kk + jnp.concatenate([kr] * N_HEADS, axis=1)).astype(BF16)
    vt = (_dot_nt(wvt_ref[...], ckvn) + one_ref[...]).astype(BF16)
    for hd in range(N_HEADS):
        sl = slice(hd * LANES, (hd + 1) * LANES)
        q_ref[0, hd] = q[:, sl]
        k_ref[0, hd] = k[:, sl]
        vt_ref[0, hd, 0] = vt[sl, :]
        kh = k[:, sl].astype(F32)
        k2 = jnp.max(jnp.sum(kh * kh, axis=1, keepdims=True), axis=0, keepdims=True)
        va = jnp.max(jnp.max(jnp.abs(vt[sl, :].astype(F32)), axis=1, keepdims=True), axis=0, keepdims=True)
        st_ref[0, 0, hd:hd + 1, :] = jnp.broadcast_to(k2, (1, LANES))
        st_ref[0, 0, N_HEADS + hd:N_HEADS + hd + 1, :] = jnp.broadcast_to(va, (1, LANES))


def _in_proj(x, mod, prep, tm):
    B, S, _ = x.shape
    grid = (B, S // tm)
    row = lambda b, i: (b, i, 0)
    hrow = lambda b, i: (b, 0, i, 0)
    head_shape = jax.ShapeDtypeStruct((B, N_HEADS, S, LANES), BF16)
    hw = N_HEADS * LANES
    return pl.pallas_call(
        _in_proj_kernel,
        out_shape=(
            head_shape, head_shape,
            jax.ShapeDtypeStruct((B, N_HEADS, S // tm, LANES, tm), BF16),
            jax.ShapeDtypeStruct((B, S, 2 * D_MLSTM), BF16),
            jax.ShapeDtypeStruct((B, D_MLSTM, S), BF16),
            jax.ShapeDtypeStruct((B, S, D_MLSTM), F32),
            jax.ShapeDtypeStruct((B, S, LANES), F32),
            jax.ShapeDtypeStruct((B, S // tm, 2 * N_HEADS, LANES), F32),
        ),
        grid=grid,
        in_specs=[pl.BlockSpec((1, tm, D_MODEL), row)] + _halo_specs(tm, S, D_MODEL) + [
            pl.BlockSpec((1, 1, 6 * D_MODEL), lambda b, i: (b, 0, 0)),
            _const_spec((1, D_MODEL)),
            _const_spec((D_MODEL, _N_PREP)),
            _const_spec((1, LANES)),
            _const_spec((1, Q_RANK)),
            _const_spec((1, KV_RANK)),
            _const_spec((Q_RANK, 2 * hw)),
            _const_spec((KV_RANK, hw)),
            _const_spec((hw, KV_RANK)),
            pl.BlockSpec((tm, LANES), lambda b, i: (i, 0)),
            pl.BlockSpec((tm, LANES), lambda b, i: (i, 0)),
            _const_spec((hw, 1)),
            _const_spec((3, 2 * D_MLSTM)),
            _const_spec((1, 2 * D_MLSTM)),
            _const_spec((D_MLSTM, D_MODEL)),
        ],
        out_specs=(
            pl.BlockSpec((1, N_HEADS, tm, LANES), hrow),
            pl.BlockSpec((1, N_HEADS, tm, LANES), hrow),
            pl.BlockSpec((1, N_HEADS, 1, LANES, tm), lambda b, i: (b, 0, i, 0, 0)),
            pl.BlockSpec((1, tm, 2 * D_MLSTM), row),
            pl.BlockSpec((1, D_MLSTM, tm), lambda b, i: (b, 0, i)),
            pl.BlockSpec((1, tm, D_MLSTM), row),
            pl.BlockSpec((1, tm, LANES), row),
            pl.BlockSpec((1, 1, 2 * N_HEADS, LANES), lambda b, i: (b, i, 0, 0)),
        ),
        compiler_params=_params(("parallel", "parallel")),
        name="in_proj",
    )(x, x, x, mod, prep["norm1_g"], prep["w_in"], prep["b_gate"], prep["q_norm_g"], prep["kv_norm_g"],
      prep["w_q"], prep["w_k"], prep["w_vt"], prep["cos"], prep["sin"], prep["v_one"],
      prep["conv_m_w"], prep["conv_m_b"], prep["w_vm_t"])


def _attn_kernel(thr_ref, q_ref, k_ref, vt_ref, o_ref, s_scr, *, sub):
    b = pl.program_id(0)
    hg = pl.program_id(1)
    hps = q_ref.shape[1]
    tq = q_ref.shape[2]
    ck = vt_ref.shape[4]
    tk = sub * ck
    n = k_ref.shape[2] // tk

    qtf = [q_ref[0, hh].astype(F32).T for hh in range(hps)]
    qts = [x.astype(BF16) for x in qtf]
    safe = None
    for hh in range(hps):
        q2 = jnp.max(jnp.sum(qtf[hh] * qtf[hh], axis=0, keepdims=True))
        ok = q2 <= thr_ref[b, hg * hps + hh]
        safe = ok if safe is None else jnp.logical_and(safe, ok)

    def value_dot(hh, c, p, acc=None):
        for u in range(sub):
            pv = _dot(vt_ref[0, hh, c * sub + u, 0:PV_ROWS, :], p[u * ck:(u + 1) * ck])
            acc = pv if acc is None else pv + acc
        return acc

    def store(hh, acc):
        out = acc[0:D_V] / acc[D_V:D_V + 1]
        o_ref[0, hh] = jnp.concatenate([out, jnp.zeros((LANES - D_V, tq), F32)], axis=0).T.astype(BF16)

    @pl.when(safe)
    def _():
        def scores(hh, c):
            return _dot(k_ref[0, hh, c * tk:(c + 1) * tk, :], qts[hh])

        acc = [None] * hps
        ahead = [[scores(hh, c) for c in range(min(ATT_AHEAD, n))] for hh in range(hps)]
        for c in range(n):
            for hh in range(hps):
                s = ahead[hh].pop(0)
                if c + ATT_AHEAD < n:
                    ahead[hh].append(scores(hh, c + ATT_AHEAD))
                acc[hh] = value_dot(hh, c, jnp.exp2(s).astype(BF16), acc[hh])
        for hh in range(hps):
            store(hh, acc[hh])

    @pl.when(jnp.logical_not(safe))
    def _():
        for hh in range(hps):
            def scores(c, slot, hh=hh):
                off = pl.multiple_of(c * tk, tk)
                s = _dot(k_ref[0, hh, pl.ds(off, tk), :], qts[hh])
                s_scr[slot] = s
                return jnp.max(s, axis=0, keepdims=True)

            def update(c, slot, mx, m, acc, hh=hh):
                m_new = jnp.maximum(m, mx)
                p = jnp.exp2(s_scr[slot] - m_new).astype(BF16)
                return m_new, jnp.exp2(m - m_new) * acc + value_dot(hh, c, p)

            def body(jj, carry):
                m, acc, mx_a = carry
                c = 2 * jj
                mx_b = scores(c + 1, 1)
                m, acc = update(c, 0, mx_a, m, acc)
                mx_a = scores(jnp.minimum(c + 2, n - 1), 0)
                m, acc = update(c + 1, 1, mx_b, m, acc)
                return m, acc, mx_a

            init = (jnp.full((1, tq), -jnp.inf, F32), jnp.zeros((PV_ROWS, tq), F32), scores(0, 0))
            _, acc, _ = lax.fori_loop(0, n // 2, body, init)
            store(hh, acc)


def _attention(q, k, vt, stats, tq, sub):
    B, H, S, _ = q.shape
    ck = vt.shape[4]
    tk = sub * ck
    assert S % (2 * tk) == 0 and S % tq == 0
    kmax2 = jnp.max(stats[:, :, 0:N_HEADS, 0], axis=1)
    vmax = jnp.max(stats[:, :, N_HEADS:, 0], axis=1)
    thr = jnp.where(vmax <= VALUE_LIMIT, SCORE_LIMIT ** 2 / jnp.maximum(kmax2, 1e-30), -1.0)
    grid_spec = pltpu.PrefetchScalarGridSpec(
        num_scalar_prefetch=1,
        grid=(B, H // ATT_HPS, S // tq),
        in_specs=[
            pl.BlockSpec((1, ATT_HPS, tq, LANES), lambda b, h, i, thr: (b, h, i, 0)),
            pl.BlockSpec((1, ATT_HPS, S, LANES), lambda b, h, i, thr: (b, h, 0, 0)),
            pl.BlockSpec((1, ATT_HPS, S // ck, LANES, ck), lambda b, h, i, thr: (b, h, 0, 0, 0)),
        ],
        out_specs=pl.BlockSpec((1, ATT_HPS, tq, LANES), lambda b, h, i, thr: (b, h, i, 0)),
        scratch_shapes=[pltpu.VMEM((2, tk, tq), F32)],
    )
    return pl.pallas_call(
        functools.partial(_attn_kernel, sub=sub),
        out_shape=jax.ShapeDtypeStruct((B, H, S, LANES), BF16),
        grid_spec=grid_spec,
        compiler_params=_params(("parallel", "parallel", "parallel")),
        name="attention",
    )(thr, q, k, vt)


def _split3(x):
    hi = x.astype(BF16)
    r1 = x - hi.astype(F32)
    mid = r1.astype(BF16)
    lo = (r1 - mid.astype(F32)).astype(BF16)
    return hi, mid, lo


def _mlstm_chunks(directions):
    L = CHUNK
    si = lax.broadcasted_iota(jnp.int32, (L, L), 0)
    ti = lax.broadcasted_iota(jnp.int32, (L, L), 1)
    one_rows = jnp.where(lax.broadcasted_iota(jnp.int32, (ST_ROWS - M_DIM, L), 0) == 0, 1.0, 0.0).astype(BF16)

    units = [d[:6] + (sub, d[7]) for d in directions for sub in d[6]]
    gates = []
    for (_, _, g_ref, _, _, _, sub, reverse) in units:
        g = g_ref[0, sub * L:(sub + 1) * L, :]
        gt = g.T
        vis = (si >= ti) if reverse else (si <= ti)
        vis_bf = jnp.where(vis, 1.0, 0.0).astype(BF16)
        vis_t_bf = jnp.where((ti >= si) if reverse else (ti <= si), 1.0, 0.0).astype(BF16)
        b_cols = sum(_dot(vis_t_bf, piece) for piece in _split3(g))
        b_rows = sum(_dot(piece, vis_bf) for piece in _split3(gt))
        gates.append((g, gt, vis, b_cols, b_rows))

    chains = [(ui, hd) for ui in range(len(units)) for hd in range(M_HEADS)]
    n = range(len(chains))
    qb, kb, vt_ext = [], [], []
    for ui, hd in chains:
        qk_ref, vt_ref, _, _, _, _, sub, _ = units[ui]
        rs = slice(sub * L, (sub + 1) * L)
        sl = slice(hd * M_DIM, (hd + 1) * M_DIM)
        qb.append(qk_ref[0, rs, sl])
        kb.append(qk_ref[0, rs, D_MLSTM + hd * M_DIM:D_MLSTM + (hd + 1) * M_DIM])
        vt_ext.append(jnp.concatenate([vt_ref[0, sl, rs], one_rows], axis=0))

    qk_t = [_dot_nt(kb[c], qb[c]) for c in n]

    b_r, dmat, a, b_last = [], [], [], []
    for ui, hd in chains:
        g, gt, vis, b_cols, b_rows = gates[ui]
        reverse = units[ui][7]
        i_off = 2 * M_HEADS if reverse else 0
        f_off = i_off + M_HEADS
        last = 0 if reverse else L - 1
        br = b_rows[f_off + hd:f_off + hd + 1, :]
        u_c = g[:, i_off + hd:i_off + hd + 1] - b_cols[:, f_off + hd:f_off + hd + 1]
        b_r.append(br)
        dmat.append(jnp.where(vis, br + u_c, -jnp.inf))
        b_last.append(br[:, last:last + 1])
        a.append(br[:, last:last + 1] - br + gt[i_off + hd:i_off + hd + 1, :])

    order = {}
    for c, (ui, hd) in enumerate(chains):
        order.setdefault((id(units[ui][4]), hd), []).append(c)
    m_prev, m_new = [None] * len(chains), [None] * len(chains)
    for cs in order.values():
        m = units[chains[cs[0]][0]][5][chains[cs[0]][1]]
        for c in cs:
            m_prev[c] = m
            m = jnp.maximum(b_last[c] + m, jnp.max(a[c], axis=1, keepdims=True))
            m_new[c] = m

    s_bf, w_state, m_t, vw, decay = [], [], [], [], []
    for c in n:
        inter = b_r[c] + m_prev[c]
        m_t.append(jnp.maximum(inter, jnp.max(dmat[c], axis=0, keepdims=True)))
        w_state.append(jnp.exp(inter - m_t[c]))
        s_bf.append((qk_t[c] * jnp.exp(dmat[c] - m_t[c])).astype(BF16))
        w_s = jnp.exp(a[c] - m_new[c])
        decay.append(jnp.exp(b_last[c] + m_prev[c] - m_new[c]))
        vw.append((vt_ext[c].astype(F32) * w_s).astype(BF16))

    upd = [_dot(vw[c], kb[c]) for c in n]

    c_prev = [None] * len(chains)
    for cs in order.values():
        ui, hd = chains[cs[0]]
        c_scr, m_scr = units[ui][4], units[ui][5]
        state = c_scr[hd]
        for c in cs:
            c_prev[c] = state
            state = decay[c] * state + upd[c]
        c_scr[hd] = state
        m_scr[hd] = m_new[cs[-1]]

    cq = [_dot_nt(c_prev[c].astype(BF16), qb[c]) for c in n]
    tot = [_dot(vt_ext[c], s_bf[c]) + w_state[c] * cq[c] for c in n]
    for c, (ui, hd) in enumerate(chains):
        h_ref, sub = units[ui][3], units[ui][6]
        den = tot[c][M_DIM:M_DIM + 1, :]
        h_t = tot[c][0:M_DIM] / jnp.maximum(jnp.abs(den), jnp.exp(-m_t[c]))
        h_ref[0, sub * L:(sub + 1) * L, hd * M_DIM:(hd + 1) * M_DIM] = h_t.T


def _mlstm_kernel(qkf_ref, vf_ref, gf_ref, qkb_ref, vb_ref, gb_ref,
                  hf_ref, hb_ref, cf_scr, mf_scr, cb_scr, mb_scr):
    @pl.when(pl.program_id(1) == 0)
    def _():
        for r in (cf_scr, mf_scr, cb_scr, mb_scr):
            r[...] = jnp.zeros_like(r)

    subs = list(range(M_CPS))
    _mlstm_chunks([(qkf_ref, vf_ref, gf_ref, hf_ref, cf_scr, mf_scr, subs, False),
                   (qkb_ref, vb_ref, gb_ref, hb_ref, cb_scr, mb_scr, subs[::-1], True)])


def _mlstm(qk_act, v_m, gates):
    B, S, _ = qk_act.shape
    rows = M_CPS * CHUNK
    nc = S // rows

    def specs(cidx):
        main = lambda b, i: (b, cidx(i), 0)
        return [
            pl.BlockSpec((1, rows, 2 * D_MLSTM), main),
            pl.BlockSpec((1, D_MLSTM, rows), lambda b, i: (b, 0, cidx(i))),
            pl.BlockSpec((1, rows, LANES), main),
        ]

    fwd = lambda i: i
    bwd = lambda i: nc - 1 - i
    state = [pltpu.VMEM((M_HEADS, ST_ROWS, M_DIM), F32), pltpu.VMEM((M_HEADS, 1, 1), F32)]
    h_shape = jax.ShapeDtypeStruct((B, S, D_MLSTM), F32)
    return pl.pallas_call(
        _mlstm_kernel,
        out_shape=(h_shape, h_shape),
        grid=(B, nc),
        in_specs=specs(fwd) + specs(bwd),
        out_specs=(
            pl.BlockSpec((1, rows, D_MLSTM), lambda b, i: (b, i, 0)),
            pl.BlockSpec((1, rows, D_MLSTM), lambda b, i: (b, nc - 1 - i, 0)),
        ),
        scratch_shapes=state + state,
        compiler_params=_params(("parallel", "arbitrary")),
        name="mlstm",
    )(qk_act, v_m, gates, qk_act, v_m, gates)


def _out_proj_kernel(x_ref, mod_ref, a_ref, hf_ref, hb_ref, om_ref, mg_ref, wa_ref, wm_ref, o_ref):
    g1 = mod_ref[0, :, 2 * D_MODEL:3 * D_MODEL]
    attn = jnp.concatenate([a_ref[0, hd] for hd in range(N_HEADS)], axis=1)
    y = _dot(attn, wa_ref[...])
    h = hf_ref[0] + hb_ref[0]
    om = om_ref[0]
    parts = []
    for hd in range(M_HEADS):
        sl = slice(hd * M_DIM, (hd + 1) * M_DIM)
        hh = h[:, sl]
        mu = jnp.mean(hh, axis=1, keepdims=True)
        d = hh - mu
        var = jnp.mean(d * d, axis=1, keepdims=True)
        parts.append(d * lax.rsqrt(var + EPS) * mg_ref[:, sl] * jax.nn.sigmoid(om[:, sl]))
    mem = jnp.concatenate(parts, axis=1).astype(BF16)
    y = y + _dot(mem, wm_ref[...])
    o_ref[0] = x_ref[0] + g1 * y


def _out_proj(x, mod, attn, h_f, h_b, o_m, prep, tm):
    B, S, _ = x.shape
    row = lambda b, i: (b, i, 0)
    return pl.pallas_call(
        _out_proj_kernel,
        out_shape=jax.ShapeDtypeStruct((B, S, D_MODEL), F32),
        grid=(B, S // tm),
        in_specs=[
            pl.BlockSpec((1, tm, D_MODEL), row),
            pl.BlockSpec((1, 1, 6 * D_MODEL), lambda b, i: (b, 0, 0)),
            pl.BlockSpec((1, N_HEADS, tm, LANES), lambda b, i: (b, 0, i, 0)),
            pl.BlockSpec((1, tm, D_MLSTM), row),
            pl.BlockSpec((1, tm, D_MLSTM), row),
            pl.BlockSpec((1, tm, D_MLSTM), row),
            _const_spec((1, D_MLSTM)),
            _const_spec((N_HEADS * LANES, D_MODEL)),
            _const_spec((D_MLSTM, D_MODEL)),
        ],
        out_specs=pl.BlockSpec((1, tm, D_MODEL), row),
        compiler_params=_params(("parallel", "parallel")),
        name="out_proj",
    )(x, mod, attn, h_f, h_b, o_m, prep["mh_norm_g"], prep["w_out_attn"], prep["w_out_mem"])


def _ffn_kernel(x_ref, xp_ref, xn_ref, mod_ref, n2_ref, wu_ref, cw_ref, cb_ref, wd_ref, fg_ref, o_ref):
    i = pl.program_id(1)
    nt = pl.num_programs(1)
    tm = x_ref.shape[1]
    sh2 = mod_ref[0, :, 3 * D_MODEL:4 * D_MODEL]
    sc2 = mod_ref[0, :, 4 * D_MODEL:5 * D_MODEL]
    g2 = mod_ref[0, :, 5 * D_MODEL:6 * D_MODEL]

    def norm_mod(r):
        return _rmsnorm(r, n2_ref[...]) * (1.0 + sc2) + sh2

    x = x_ref[0]
    hp = jnp.where(i > 0, norm_mod(xp_ref[0]), 0.0)
    hn = jnp.where(i < nt - 1, norm_mod(xn_ref[0]), 0.0)
    hext = jnp.concatenate([hp, norm_mod(x), hn], axis=0).astype(BF16)

    n_chunks = D_FF // FF_CHUNK
    ca = [slice(j * FF_CHUNK, (j + 1) * FF_CHUNK) for j in range(n_chunks)]
    cg = [slice(D_FF + j * FF_CHUNK, D_FF + (j + 1) * FF_CHUNK) for j in range(n_chunks)]

    def up(j):
        return _dot(hext, wu_ref[:, ca[j]]), _dot(hext, wu_ref[:, cg[j]])

    acc = jnp.zeros((tm, D_MODEL), F32)
    ahead = [up(j) for j in range(min(FF_AHEAD, n_chunks))]
    for j in range(n_chunks):
        ua, ug = ahead.pop(0)
        if j + FF_AHEAD < n_chunks:
            ahead.append(up(j + FF_AHEAD))
        a = _conv3(ua, cw_ref, cb_ref, ca[j], tm)
        g = _conv3(ug, cw_ref, cb_ref, cg[j], tm)
        act = (g * jax.nn.sigmoid(g) * a).astype(BF16)
        acc = acc + _dot(act, wd_ref[ca[j], :])
    o_ref[0] = _rmsnorm(x + g2 * acc, fg_ref[...])


def _conv_ffn(x, mod, prep, tm):
    B, S, _ = x.shape
    row = lambda b, i: (b, i, 0)
    return pl.pallas_call(
        _ffn_kernel,
        out_shape=jax.ShapeDtypeStruct((B, S, D_MODEL), F32),
        grid=(B, S // tm),
        in_specs=[pl.BlockSpec((1, tm, D_MODEL), row)] + _halo_specs(tm, S, D_MODEL) + [
            pl.BlockSpec((1, 1, 6 * D_MODEL), lambda b, i: (b, 0, 0)),
            _const_spec((1, D_MODEL)),
            _const_spec((D_MODEL, 2 * D_FF)),
            _const_spec((3, 2 * D_FF)),
            _const_spec((1, 2 * D_FF)),
            _const_spec((D_FF, D_MODEL)),
            _const_spec((1, D_MODEL)),
        ],
        out_specs=pl.BlockSpec((1, tm, D_MODEL), row),
        compiler_params=_params(("parallel", "parallel")),
        name="conv_ffn",
    )(x, x, x, mod, prep["norm2_g"], prep["w_up"], prep["conv_f_w"], prep["conv_f_b"],
      prep["w_down"], prep["final_g"])


def _head_groups(w, n_heads, width, pieces):
    rows = w.shape[0]
    wh = w.reshape(rows, n_heads, width)
    out = jnp.zeros((rows, n_heads, LANES), F32)
    for src, n, dst, sign in pieces:
        out = out.at[:, :, dst:dst + n].set(sign * wh[:, :, src:src + n])
    return out.reshape(rows, n_heads * LANES)


def _prepare(S, norm1_g, w_in, b_gate, q_norm_g, kv_norm_g, w_uq, w_ukv, conv_m_w, conv_m_b,
             mh_norm_g, w_out, norm2_g, w_up, conv_f_w, conv_f_b, w_down, final_g):
    half = D_ROPE // 2
    r1 = D_NOPE
    r2 = D_NOPE + half
    c1, c2 = Q_RANK + KV_RANK, Q_RANK + KV_RANK + D_ROPE
    c3 = c2 + 2 * D_MLSTM
    c4 = c3 + D_MLSTM
    c5 = c4 + D_MLSTM

    kr = w_in[:, c1:c2]
    zeros = lambda n: jnp.zeros((D_MODEL, n), F32)
    kr_main = jnp.concatenate([zeros(r1), kr[:, :half], kr[:, half:], zeros(LANES - r2 - half)], axis=1)
    kr_swap = jnp.concatenate([zeros(r1), -kr[:, half:], kr[:, :half], zeros(LANES - r2 - half)], axis=1)
    gates = jnp.concatenate([w_in[:, c5:], zeros(LANES - 4 * M_HEADS)], axis=1)
    w_prep = jnp.concatenate([w_in[:, :c1], kr_main, kr_swap, w_in[:, c2:c3], w_in[:, c4:c5], gates], axis=1)

    qs = (D_NOPE + D_ROPE) ** -0.5 * math.log2(math.e)
    dq = D_NOPE + D_ROPE
    wq_main = _head_groups(w_uq, N_HEADS, dq, [(0, D_NOPE, 0, 1.0), (D_NOPE, half, r1, 1.0),
                                                (D_NOPE + half, half, r2, 1.0)])
    wq_swap = _head_groups(w_uq, N_HEADS, dq, [(D_NOPE + half, half, r1, -1.0), (D_NOPE, half, r2, 1.0)])
    w_q = jnp.concatenate([wq_main, wq_swap], axis=1) * qs

    dkv = D_NOPE + D_V
    wk = _head_groups(w_ukv, N_HEADS, dkv, [(0, D_NOPE, 0, 1.0)])
    wv = _head_groups(w_ukv, N_HEADS, dkv, [(D_NOPE, D_V, 0, 1.0)])

    inv = 1.0 / (ROPE_THETA ** (jnp.arange(half, dtype=F32) * (2.0 / D_ROPE)))
    ang = jnp.arange(S, dtype=jnp.int32).astype(F32)[:, None] * inv[None, :]
    ones = jnp.ones((S, r1), F32)
    pad = jnp.zeros((S, LANES - r2 - half), F32)
    cos = jnp.concatenate([ones, jnp.cos(ang), jnp.cos(ang), pad], axis=1)
    sin = jnp.concatenate([0.0 * ones, jnp.sin(ang), jnp.sin(ang), pad], axis=1)

    v_one = jnp.zeros((N_HEADS, LANES), F32).at[:, D_V].set(1.0).reshape(N_HEADS * LANES, 1)
    b_g = jnp.concatenate([b_gate, jnp.zeros((LANES - 4 * M_HEADS,), F32)]).reshape(1, LANES)

    d_attn = N_HEADS * D_V
    w_oa = jnp.zeros((N_HEADS, LANES, D_MODEL), F32).at[:, :D_V, :].set(
        w_out[:d_attn].reshape(N_HEADS, D_V, D_MODEL)).reshape(N_HEADS * LANES, D_MODEL)

    return dict(
        norm1_g=norm1_g.reshape(1, -1), w_in=w_prep.astype(BF16), w_vm_t=w_in[:, c3:c4].T.astype(BF16), b_gate=b_g,
        q_norm_g=q_norm_g.reshape(1, -1), kv_norm_g=kv_norm_g.reshape(1, -1),
        w_q=w_q.astype(BF16), w_k=wk.astype(BF16), w_vt=wv.T.astype(BF16), cos=cos, sin=sin, v_one=v_one,
        conv_m_w=conv_m_w, conv_m_b=conv_m_b.reshape(1, -1), mh_norm_g=mh_norm_g.reshape(1, -1),
        w_out_attn=w_oa.astype(BF16), w_out_mem=w_out[d_attn:].astype(BF16),
        norm2_g=norm2_g.reshape(1, -1), w_up=w_up.astype(BF16), conv_f_w=conv_f_w,
        conv_f_b=conv_f_b.reshape(1, -1), w_down=w_down.astype(BF16), final_g=final_g.reshape(1, -1),
    )


def _tiles(S):
    tm = min(512, S // 2)
    sub = 2 if S % (4 * tm) == 0 else 1
    return tm, min(512, S), sub


def _trunk(x, c, w_ada, b_ada, prep):
    S = x.shape[1]
    tm, tq, sub = _tiles(S)
    mod = _ada_mod(c, w_ada, b_ada)
    q, k, vt, qk_act, v_m, o_m, gates, stats = _in_proj(x, mod, prep, tm)
    attn = _attention(q, k, vt, stats, tq, sub)
    h_f, h_b = _mlstm(qk_act, v_m, gates)
    x1 = _out_proj(x, mod, attn, h_f, h_b, o_m, prep, tm)
    return _conv_ffn(x1, mod, prep, min(FFN_ROWS, tm))


def kernel(x_prompt, x_sample, c_prompt, c_sample, norm1_g, w_ada, b_ada, w_in, b_gate, q_norm_g, kv_norm_g, w_uq, w_ukv, conv_m_w, conv_m_b, mh_norm_g, w_out, norm2_g, w_up, conv_f_w, conv_f_b, w_down, final_g):
    assert w_ada.shape[0] == 1, "single-layer trunk"
    assert x_prompt.shape[1] == x_sample.shape[1]
    prep = _prepare(x_prompt.shape[1], norm1_g[0], w_in[0], b_gate[0], q_norm_g[0], kv_norm_g[0], w_uq[0],
                    w_ukv[0], conv_m_w[0], conv_m_b[0], mh_norm_g[0], w_out[0], norm2_g[0], w_up[0],
                    conv_f_w[0], conv_f_b[0], w_down[0], final_g)
    y_prompt = _trunk(x_prompt, c_prompt, w_ada[0], b_ada[0], prep)
    y_sample = _trunk(x_sample, c_sample, w_ada[0], b_ada[0], prep)
    return (y_prompt, y_sample)
```

```python
import functools
import math

import jax
import jax.numpy as jnp
from jax import lax
from jax.experimental import pallas as pl
from jax.experimental.pallas import tpu as pltpu

F32 = jnp.float32
BF16 = jnp.bfloat16

D_MODEL = 1024
N_HEADS = 8
D_NOPE = 64
D_ROPE = 32
D_V = 64
Q_RANK = 384
KV_RANK = 256
M_HEADS = 4
M_DIM = 128
D_MLSTM = M_HEADS * M_DIM
CHUNK = 128
M_CPS = 8
D_FF = 2816
ROPE_THETA = 10000.0
EPS = 1e-6
LANES = 128
SUBLANES = 8
FF_CHUNK = 256
FF_AHEAD = 2
FFN_ROWS = 256
ST_ROWS = M_DIM + 16
ATT_HPS = 2
ATT_AHEAD = 1
SCORE_LIMIT = 40.0
VALUE_LIMIT = 2.0 ** 60
PV_ROWS = 80
VMEM_LIMIT = 56 * 1024 * 1024

_A_W = Q_RANK + KV_RANK + 2 * LANES
_QK0 = _A_W
_O0 = _QK0 + 2 * D_MLSTM
_G0 = _O0 + D_MLSTM
_N_PREP = _G0 + LANES


def _dot(a, b):
    return jnp.dot(a, b, preferred_element_type=F32)


def _dot_nt(a, b):
    return lax.dot_general(a, b, (((1,), (1,)), ((), ())), preferred_element_type=F32)


def _dot_tn(a, b):
    return lax.dot_general(a, b, (((0,), (0,)), ((), ())), preferred_element_type=F32)


def _rmsnorm(x, g):
    return x * lax.rsqrt(jnp.mean(x * x, axis=-1, keepdims=True) + EPS) * g


def _log_sigmoid(x):
    return jnp.minimum(x, 0.0) - jnp.log1p(jnp.exp(-jnp.abs(x)))


def _params(sem):
    return pltpu.CompilerParams(dimension_semantics=sem, vmem_limit_bytes=VMEM_LIMIT)


def _const_spec(shape):
    nd = len(shape)
    return pl.BlockSpec(shape, lambda *_: (0,) * nd, pipeline_mode=pl.Buffered(1))


def _halo_specs(tm, S, width):
    per = tm // SUBLANES
    nsub = S // SUBLANES
    return [
        pl.BlockSpec((1, SUBLANES, width), lambda b, i: (b, jnp.maximum(i * per - 1, 0), 0)),
        pl.BlockSpec((1, SUBLANES, width), lambda b, i: (b, jnp.minimum((i + 1) * per, nsub - 1), 0)),
    ]


def _conv3(u, cw_ref, cb_ref, cols, tm):
    rows = tm + 2 * SUBLANES
    full = (cw_ref[0:1, cols] * pltpu.roll(u, 1, axis=0) + cw_ref[1:2, cols] * u
            + cw_ref[2:3, cols] * pltpu.roll(u, rows - 1, axis=0) + cb_ref[:, cols])
    return full[SUBLANES:SUBLANES + tm]


def _ada_kernel(c_ref, w_ref, b_ref, o_ref):
    c = c_ref[...]
    a = (c * jax.nn.sigmoid(c)).astype(BF16)
    o_ref[...] = _dot(a, w_ref[...].astype(BF16)) + b_ref[...]


def _ada_mod(c, w_ada, b_ada):
    B = c.shape[0]
    cp = jnp.zeros((SUBLANES, D_MODEL), F32).at[:B].set(c)
    n = 6 * D_MODEL
    out = pl.pallas_call(
        _ada_kernel,
        out_shape=jax.ShapeDtypeStruct((SUBLANES, n), F32),
        grid=(6,),
        in_specs=[
            pl.BlockSpec((SUBLANES, D_MODEL), lambda j: (0, 0)),
            pl.BlockSpec((D_MODEL, D_MODEL), lambda j: (0, j)),
            pl.BlockSpec((1, D_MODEL), lambda j: (0, j)),
        ],
        out_specs=pl.BlockSpec((SUBLANES, D_MODEL), lambda j: (0, j)),
        compiler_params=_params(("parallel",)),
        name="ada_mod",
    )(cp, w_ada, b_ada.reshape(1, n))
    return out[:B].reshape(B, 1, n)


def _in_proj_kernel(x_ref, xp_ref, xn_ref, mod_ref, n1_ref, w_ref, bg_ref, qn_ref, kvn_ref, wq_ref, wk_ref,
                    wvt_ref, cos_ref, sin_ref, one_ref, cw_ref, cb_ref, wvm_ref,
                    q_ref, k_ref, vt_ref, qk_ref, vm_ref, om_ref, g_ref, st_ref):
    i = pl.program_id(1)
    nt = pl.num_programs(1)
    tm = x_ref.shape[1]
    sh1 = mod_ref[0, :, 0:D_MODEL]
    sc1 = mod_ref[0, :, D_MODEL:2 * D_MODEL]

    def norm_mod(r):
        return _rmsnorm(r, n1_ref[...]) * (1.0 + sc1) + sh1

    hf = norm_mod(x_ref[0])
    h = hf.astype(BF16)
    hp = jnp.where(i > 0, norm_mod(xp_ref[0]), 0.0)
    hn = jnp.where(i < nt - 1, norm_mod(xn_ref[0]), 0.0)
    hext = jnp.concatenate([hp, hf, hn], axis=0).astype(BF16)
    pa = _dot(h, w_ref[:, 0:_A_W])
    vm_ref[0] = _dot_nt(wvm_ref[...], h).astype(BF16)
    om_ref[0] = _dot(h, w_ref[:, _O0:_G0])
    g = _dot(h, w_ref[:, _G0:_N_PREP]) + bg_ref[...]
    lane = lax.broadcasted_iota(jnp.int32, g.shape, 1)
    is_forget = (lane < 4 * M_HEADS) & ((lane // M_HEADS) % 2 == 1)
    g_ref[0] = jnp.where(is_forget, _log_sigmoid(g), g)

    conv = _conv3(_dot(hext, w_ref[:, _QK0:_O0]), cw_ref, cb_ref, slice(None), tm)
    qk = conv * jax.nn.sigmoid(conv)
    qk_ref[0, :, 0:D_MLSTM] = (qk[:, 0:D_MLSTM] * (M_DIM ** -0.5)).astype(BF16)
    qk_ref[0, :, D_MLSTM:] = qk[:, D_MLSTM:].astype(BF16)

    cos = cos_ref[...]
    sin = sin_ref[...]
    cos8 = jnp.concatenate([cos] * N_HEADS, axis=1)
    sin8 = jnp.concatenate([sin] * N_HEADS, axis=1)
    hw = N_HEADS * LANES

    cqn = _rmsnorm(pa[:, 0:Q_RANK], qn_ref[...]).astype(BF16)
    qq = _dot(cqn, wq_ref[...])
    q = (qq[:, 0:hw] * cos8 + qq[:, hw:2 * hw] * sin8).astype(BF16)

    ckvn = _rmsnorm(pa[:, Q_RANK:Q_RANK + KV_RANK], kvn_ref[...]).astype(BF16)
    kk = _dot(ckvn, wk_ref[...])
    kr0 = Q_RANK + KV_RANK
    kr = pa[:, kr0:kr0 + LANES] * cos + pa[:, kr0 + LANES:kr0 + 2 * LANES] * sin
    k = (kk + jnp.concatenate([kr] * N_HEADS, axis=1)).astype(BF16)
    vt = (_dot_nt(wvt_ref[...], ckvn) + one_ref[...]).astype(BF16)
    for hd in range(N_HEADS):
        sl = slice(hd * LANES, (hd + 1) * LANES)
        q_ref[0, hd] = q[:, sl]
        k_ref[0, hd] = k[:, sl]
        vt_ref[0, hd, 0] = vt[sl, :]
        kh = k[:, sl].astype(F32)
        k2 = jnp.max(jnp.sum(kh * kh, axis=1, keepdims=True), axis=0, keepdims=True)
        va = jnp.max(jnp.max(jnp.abs(vt[sl, :].astype(F32)), axis=1, keepdims=True), axis=0, keepdims=True)
        st_ref[0, 0, hd:hd + 1, :] = jnp.broadcast_to(k2, (1, LANES))
        st_ref[0, 0, N_HEADS + hd:N_HEADS + hd + 1, :] = jnp.broadcast_to(va, (1, LANES))


def _in_proj(x, mod, prep, tm):
    B, S, _ = x.shape
    grid = (B, S // tm)
    row = lambda b, i: (b, i, 0)
    hrow = lambda b, i: (b, 0, i, 0)
    head_shape = jax.ShapeDtypeStruct((B, N_HEADS, S, LANES), BF16)
    hw = N_HEADS * LANES
    return pl.pallas_call(
        _in_proj_kernel,
        out_shape=(
            head_shape, head_shape,
            jax.ShapeDtypeStruct((B, N_HEADS, S // tm, LANES, tm), BF16),
            jax.ShapeDtypeStruct((B, S, 2 * D_MLSTM), BF16),
            jax.ShapeDtypeStruct((B, D_MLSTM, S), BF16),
            jax.ShapeDtypeStruct((B, S, D_MLSTM), F32),
            jax.ShapeDtypeStruct((B, S, LANES), F32),
            jax.ShapeDtypeStruct((B, S // tm, 2 * N_HEADS, LANES), F32),
        ),
        grid=grid,
        in_specs=[pl.BlockSpec((1, tm, D_MODEL), row)] + _halo_specs(tm, S, D_MODEL) + [
            pl.BlockSpec((1, 1, 6 * D_MODEL), lambda b, i: (b, 0, 0)),
            _const_spec((1, D_MODEL)),
            _const_spec((D_MODEL, _N_PREP)),
            _const_spec((1, LANES)),
            _const_spec((1, Q_RANK)),
            _const_spec((1, KV_RANK)),
            _const_spec((Q_RANK, 2 * hw)),
            _const_spec((KV_RANK, hw)),
            _const_spec((hw, KV_RANK)),
            pl.BlockSpec((tm, LANES), lambda b, i: (i, 0)),
            pl.BlockSpec((tm, LANES), lambda b, i: (i, 0)),
            _const_spec((hw, 1)),
            _const_spec((3, 2 * D_MLSTM)),
            _const_spec((1, 2 * D_MLSTM)),
            _const_spec((D_MLSTM, D_MODEL)),
        ],
        out_specs=(
            pl.BlockSpec((1, N_HEADS, tm, LANES), hrow),
            pl.BlockSpec((1, N_HEADS, tm, LANES), hrow),
            pl.BlockSpec((1, N_HEADS, 1, LANES, tm), lambda b, i: (b, 0, i, 0, 0)),
            pl.BlockSpec((1, tm, 2 * D_MLSTM), row),
            pl.BlockSpec((1, D_MLSTM, tm), lambda b, i: (b, 0, i)),
            pl.BlockSpec((1, tm, D_MLSTM), row),
            pl.BlockSpec((1, tm, LANES), row),
            pl.BlockSpec((1, 1, 2 * N_HEADS, LANES), lambda b, i: (b, i, 0, 0)),
        ),
        compiler_params=_params(("parallel", "parallel")),
        name="in_proj",
    )(x, x, x, mod, prep["norm1_g"], prep["w_in"], prep["b_gate"], prep["q_norm_g"], prep["kv_norm_g"],
      prep["w_q"], prep["w_k"], prep["w_vt"], prep["cos"], prep["sin"], prep["v_one"],
      prep["conv_m_w"], prep["conv_m_b"], prep["w_vm_t"])


def _attn_kernel(thr_ref, q_ref, k_ref, vt_ref, o_ref, s_scr, *, sub):
    b = pl.program_id(0)
    hg = pl.program_id(1)
    hps = q_ref.shape[1]
    tq = q_ref.shape[2]
    ck = vt_ref.shape[4]
    tk = sub * ck
    n = k_ref.shape[2] // tk

    qtf = [q_ref[0, hh].astype(F32).T for hh in range(hps)]
    qts = [x.astype(BF16) for x in qtf]
    safe = None
    for hh in range(hps):
        q2 = jnp.max(jnp.sum(qtf[hh] * qtf[hh], axis=0, keepdims=True))
        ok = q2 <= thr_ref[b, hg * hps + hh]
        safe = ok if safe is None else jnp.logical_and(safe, ok)

    def value_dot(hh, c, p, acc=None):
        for u in range(sub):
            pv = _dot(vt_ref[0, hh, c * sub + u, 0:PV_ROWS, :], p[u * ck:(u + 1) * ck])
            acc = pv if acc is None else pv + acc
        return acc

    def store(hh, acc):
        out = acc[0:D_V] / acc[D_V:D_V + 1]
        o_ref[0, hh] = jnp.concatenate([out, jnp.zeros((LANES - D_V, tq), F32)], axis=0).T.astype(BF16)

    @pl.when(safe)
    def _():
        def scores(hh, c):
            return _dot(k_ref[0, hh, c * tk:(c + 1) * tk, :], qts[hh])

        acc = [None] * hps
        ahead = [[scores(hh, c) for c in range(min(ATT_AHEAD, n))] for hh in range(hps)]
        for c in range(n):
            for hh in range(hps):
                s = ahead[hh].pop(0)
                if c + ATT_AHEAD < n:
                    ahead[hh].append(scores(hh, c + ATT_AHEAD))
                acc[hh] = value_dot(hh, c, jnp.exp2(s).astype(BF16), acc[hh])
        for hh in range(hps):
            store(hh, acc[hh])

    @pl.when(jnp.logical_not(safe))
    def _():
        for hh in range(hps):
            def scores(c, slot, hh=hh):
                off = pl.multiple_of(c * tk, tk)
                s = _dot(k_ref[0, hh, pl.ds(off, tk), :], qts[hh])
                s_scr[slot] = s
                return jnp.max(s, axis=0, keepdims=True)

            def update(c, slot, mx, m, acc, hh=hh):
                m_new = jnp.maximum(m, mx)
                p = jnp.exp2(s_scr[slot] - m_new).astype(BF16)
                return m_new, jnp.exp2(m - m_new) * acc + value_dot(hh, c, p)

            def body(jj, carry):
                m, acc, mx_a = carry
                c = 2 * jj
                mx_b = scores(c + 1, 1)
                m, acc = update(c, 0, mx_a, m, acc)
                mx_a = scores(jnp.minimum(c + 2, n - 1), 0)
                m, acc = update(c + 1, 1, mx_b, m, acc)
                return m, acc, mx_a

            init = (jnp.full((1, tq), -jnp.inf, F32), jnp.zeros((PV_ROWS, tq), F32), scores(0, 0))
            _, acc, _ = lax.fori_loop(0, n // 2, body, init)
            store(hh, acc)


def _attention(q, k, vt, stats, tq, sub):
    B, H, S, _ = q.shape
    ck = vt.shape[4]
    tk = sub * ck
    assert S % (2 * tk) == 0 and S % tq == 0
    kmax2 = jnp.max(stats[:, :, 0:N_HEADS, 0], axis=1)
    vmax = jnp.max(stats[:, :, N_HEADS:, 0], axis=1)
    thr = jnp.where(vmax <= VALUE_LIMIT, SCORE_LIMIT ** 2 / jnp.maximum(kmax2, 1e-30), -1.0)
    grid_spec = pltpu.PrefetchScalarGridSpec(
        num_scalar_prefetch=1,
        grid=(B, H // ATT_HPS, S // tq),
        in_specs=[
            pl.BlockSpec((1, ATT_HPS, tq, LANES), lambda b, h, i, thr: (b, h, i, 0)),
            pl.BlockSpec((1, ATT_HPS, S, LANES), lambda b, h, i, thr: (b, h, 0, 0)),
            pl.BlockSpec((1, ATT_HPS, S // ck, LANES, ck), lambda b, h, i, thr: (b, h, 0, 0, 0)),
        ],
        out_specs=pl.BlockSpec((1, ATT_HPS, tq, LANES), lambda b, h, i, thr: (b, h, i, 0)),
        scratch_shapes=[pltpu.VMEM((2, tk, tq), F32)],
    )
    return pl.pallas_call(
        functools.partial(_attn_kernel, sub=sub),
        out_shape=jax.ShapeDtypeStruct((B, H, S, LANES), BF16),
        grid_spec=grid_spec,
        compiler_params=_params(("parallel", "parallel", "parallel")),
        name="attention",
    )(thr, q, k, vt)


def _split3(x):
    hi = x.astype(BF16)
    r1 = x - hi.astype(F32)
    mid = r1.astype(BF16)
    lo = (r1 - mid.astype(F32)).astype(BF16)
    return hi, mid, lo


def _mlstm_chunks(directions):
    L = CHUNK
    si = lax.broadcasted_iota(jnp.int32, (L, L), 0)
    ti = lax.broadcasted_iota(jnp.int32, (L, L), 1)
    one_rows = jnp.where(lax.broadcasted_iota(jnp.int32, (ST_ROWS - M_DIM, L), 0) == 0, 1.0, 0.0).astype(BF16)

    units = [d[:6] + (sub, d[7]) for d in directions for sub in d[6]]
    gates = []
    for (_, _, g_ref, _, _, _, sub, reverse) in units:
        g = g_ref[0, sub * L:(sub + 1) * L, :]
        gt = g.T
        vis = (si >= ti) if reverse else (si <= ti)
        vis_bf = jnp.where(vis, 1.0, 0.0).astype(BF16)
        vis_t_bf = jnp.where((ti >= si) if reverse else (ti <= si), 1.0, 0.0).astype(BF16)
        b_cols = sum(_dot(vis_t_bf, piece) for piece in _split3(g))
        b_rows = sum(_dot(piece, vis_bf) for piece in _split3(gt))
        gates.append((g, gt, vis, b_cols, b_rows))

    chains = [(ui, hd) for ui in range(len(units)) for hd in range(M_HEADS)]
    n = range(len(chains))
    qb, kb, vt_ext = [], [], []
    for ui, hd in chains:
        qk_ref, vt_ref, _, _, _, _, sub, _ = units[ui]
        rs = slice(sub * L, (sub + 1) * L)
        sl = slice(hd * M_DIM, (hd + 1) * M_DIM)
        qb.append(qk_ref[0, rs, sl])
        kb.append(qk_ref[0, rs, D_MLSTM + hd * M_DIM:D_MLSTM + (hd + 1) * M_DIM])
        vt_ext.append(jnp.concatenate([vt_ref[0, sl, rs], one_rows], axis=0))

    qk_t = [_dot_nt(kb[c], qb[c]) for c in n]

    b_r, dmat, a, b_last = [], [], [], []
    for ui, hd in chains:
        g, gt, vis, b_cols, b_rows = gates[ui]
        reverse = units[ui][7]
        i_off = 2 * M_HEADS if reverse else 0
        f_off = i_off + M_HEADS
        last = 0 if reverse else L - 1
        br = b_rows[f_off + hd:f_off + hd + 1, :]
        u_c = g[:, i_off + hd:i_off + hd + 1] - b_cols[:, f_off + hd:f_off + hd + 1]
        b_r.append(br)
        dmat.append(jnp.where(vis, br + u_c, -jnp.inf))
        b_last.append(br[:, last:last + 1])
        a.append(br[:, last:last + 1] - br + gt[i_off + hd:i_off + hd + 1, :])

    order = {}
    for c, (ui, hd) in enumerate(chains):
        order.setdefault((id(units[ui][4]), hd), []).append(c)
    m_prev, m_new = [None] * len(chains), [None] * len(chains)
    for cs in order.values():
        m = units[chains[cs[0]][0]][5][chains[cs[0]][1]]
        for c in cs:
            m_prev[c] = m
            m = jnp.maximum(b_last[c] + m, jnp.max(a[c], axis=1, keepdims=True))
            m_new[c] = m

    s_bf, w_state, m_t, vw, decay = [], [], [], [], []
    for c in n:
        inter = b_r[c] + m_prev[c]
        m_t.append(jnp.maximum(inter, jnp.max(dmat[c], axis=0, keepdims=True)))
        w_state.append(jnp.exp(inter - m_t[c]))
        s_bf.append((qk_t[c] * jnp.exp(dmat[c] - m_t[c])).astype(BF16))
        w_s = jnp.exp(a[c] - m_new[c])
        decay.append(jnp.exp(b_last[c] + m_prev[c] - m_new[c]))
        vw.append((vt_ext[c].astype(F32) * w_s).astype(BF16))

    upd = [_dot(vw[c], kb[c]) for c in n]

    c_prev = [None] * len(chains)
    for cs in order.values():
        ui, hd = chains[cs[0]]
        c_scr, m_scr = units[ui][4], units[ui][5]
        state = c_scr[hd]
        for c in cs:
            c_prev[c] = state
            state = decay[c] * state + upd[c]
        c_scr[hd] = state
        m_scr[hd] = m_new[cs[-1]]

    cq = [_dot_nt(c_prev[c].astype(BF16), qb[c]) for c in n]
    tot = [_dot(vt_ext[c], s_bf[c]) + w_state[c] * cq[c] for c in n]
    for c, (ui, hd) in enumerate(chains):
        h_ref, sub = units[ui][3], units[ui][6]
        den = tot[c][M_DIM:M_DIM + 1, :]
        h_t = tot[c][0:M_DIM] / jnp.maximum(jnp.abs(den), jnp.exp(-m_t[c]))
        h_ref[0, sub * L:(sub + 1) * L, hd * M_DIM:(hd + 1) * M_DIM] = h_t.T


def _mlstm_kernel(qkf_ref, vf_ref, gf_ref, qkb_ref, vb_ref, gb_ref,
                  hf_ref, hb_ref, cf_scr, mf_scr, cb_scr, mb_scr):
    @pl.when(pl.program_id(1) == 0)
    def _():
        for r in (cf_scr, mf_scr, cb_scr, mb_scr):
            r[...] = jnp.zeros_like(r)

    subs = list(range(M_CPS))
    _mlstm_chunks([(qkf_ref, vf_ref, gf_ref, hf_ref, cf_scr, mf_scr, subs, False),
                   (qkb_ref, vb_ref, gb_ref, hb_ref, cb_scr, mb_scr, subs[::-1], True)])


def _mlstm(qk_act, v_m, gates):
    B, S, _ = qk_act.shape
    rows = M_CPS * CHUNK
    nc = S // rows

    def specs(cidx):
        main = lambda b, i: (b, cidx(i), 0)
        return [
            pl.BlockSpec((1, rows, 2 * D_MLSTM), main),
            pl.BlockSpec((1, D_MLSTM, rows), lambda b, i: (b, 0, cidx(i))),
            pl.BlockSpec((1, rows, LANES), main),
        ]

    fwd = lambda i: i
    bwd = lambda i: nc - 1 - i
    state = [pltpu.VMEM((M_HEADS, ST_ROWS, M_DIM), F32), pltpu.VMEM((M_HEADS, 1, 1), F32)]
    h_shape = jax.ShapeDtypeStruct((B, S, D_MLSTM), F32)
    return pl.pallas_call(
        _mlstm_kernel,
        out_shape=(h_shape, h_shape),
        grid=(B, nc),
        in_specs=specs(fwd) + specs(bwd),
        out_specs=(
            pl.BlockSpec((1, rows, D_MLSTM), lambda b, i: (b, i, 0)),
            pl.BlockSpec((1, rows, D_MLSTM), lambda b, i: (b, nc - 1 - i, 0)),
        ),
        scratch_shapes=state + state,
        compiler_params=_params(("parallel", "arbitrary")),
        name="mlstm",
    )(qk_act, v_m, gates, qk_act, v_m, gates)


def _out_proj_kernel(x_ref, mod_ref, a_ref, hf_ref, hb_ref, om_ref, mg_ref, wa_ref, wm_ref, o_ref):
    g1 = mod_ref[0, :, 2 * D_MODEL:3 * D_MODEL]
    attn = jnp.concatenate([a_ref[0, hd] for hd in range(N_HEADS)], axis=1)
    y = _dot(attn, wa_ref[...])
    h = hf_ref[0] + hb_ref[0]
    om = om_ref[0]
    parts = []
    for hd in range(M_HEADS):
        sl = slice(hd * M_DIM, (hd + 1) * M_DIM)
        hh = h[:, sl]
        mu = jnp.mean(hh, axis=1, keepdims=True)
        d = hh - mu
        var = jnp.mean(d * d, axis=1, keepdims=True)
        parts.append(d * lax.rsqrt(var + EPS) * mg_ref[:, sl] * jax.nn.sigmoid(om[:, sl]))
    mem = jnp.concatenate(parts, axis=1).astype(BF16)
    y = y + _dot(mem, wm_ref[...])
    o_ref[0] = x_ref[0] + g1 * y


def _out_proj(x, mod, attn, h_f, h_b, o_m, prep, tm):
    B, S, _ = x.shape
    row = lambda b, i: (b, i, 0)
    return pl.pallas_call(
        _out_proj_kernel,
        out_shape=jax.ShapeDtypeStruct((B, S, D_MODEL), F32),
        grid=(B, S // tm),
        in_specs=[
            pl.BlockSpec((1, tm, D_MODEL), row),
            pl.BlockSpec((1, 1, 6 * D_MODEL), lambda b, i: (b, 0, 0)),
            pl.BlockSpec((1, N_HEADS, tm, LANES), lambda b, i: (b, 0, i, 0)),
            pl.BlockSpec((1, tm, D_MLSTM), row),
            pl.BlockSpec((1, tm, D_MLSTM), row),
            pl.BlockSpec((1, tm, D_MLSTM), row),
            _const_spec((1, D_MLSTM)),
            _const_spec((N_HEADS * LANES, D_MODEL)),
            _const_spec((D_MLSTM, D_MODEL)),
        ],
        out_specs=pl.BlockSpec((1, tm, D_MODEL), row),
        compiler_params=_params(("parallel", "parallel")),
        name="out_proj",
    )(x, mod, attn, h_f, h_b, o_m, prep["mh_norm_g"], prep["w_out_attn"], prep["w_out_mem"])


def _ffn_kernel(x_ref, xp_ref, xn_ref, mod_ref, n2_ref, wu_ref, cw_ref, cb_ref, wd_ref, fg_ref, o_ref):
    i = pl.program_id(1)
    nt = pl.num_programs(1)
    tm = x_ref.shape[1]
    sh2 = mod_ref[0, :, 3 * D_MODEL:4 * D_MODEL]
    sc2 = mod_ref[0, :, 4 * D_MODEL:5 * D_MODEL]
    g2 = mod_ref[0, :, 5 * D_MODEL:6 * D_MODEL]

    def norm_mod(r):
        return _rmsnorm(r, n2_ref[...]) * (1.0 + sc2) + sh2

    x = x_ref[0]
    hp = jnp.where(i > 0, norm_mod(xp_ref[0]), 0.0)
    hn = jnp.where(i < nt - 1, norm_mod(xn_ref[0]), 0.0)
    hext = jnp.concatenate([hp, norm_mod(x), hn], axis=0).astype(BF16)

    n_chunks = D_FF // FF_CHUNK
    ca = [slice(j * FF_CHUNK, (j + 1) * FF_CHUNK) for j in range(n_chunks)]
    cg = [slice(D_FF + j * FF_CHUNK, D_FF + (j + 1) * FF_CHUNK) for j in range(n_chunks)]

    def up(j):
        return _dot(hext, wu_ref[:, ca[j]]), _dot(hext, wu_ref[:, cg[j]])

    acc = jnp.zeros((tm, D_MODEL), F32)
    ahead = [up(j) for j in range(min(FF_AHEAD, n_chunks))]
    for j in range(n_chunks):
        ua, ug = ahead.pop(0)
        if j + FF_AHEAD < n_chunks:
            ahead.append(up(j + FF_AHEAD))
        a = _conv3(ua, cw_ref, cb_ref, ca[j], tm)
        g = _conv3(ug, cw_ref, cb_ref, cg[j], tm)
        act = (g * jax.nn.sigmoid(g) * a).astype(BF16)
        acc = acc + _dot(act, wd_ref[ca[j], :])
    o_ref[0] = _rmsnorm(x + g2 * acc, fg_ref[...])


def _conv_ffn(x, mod, prep, tm):
    B, S, _ = x.shape
    row = lambda b, i: (b, i, 0)
    return pl.pallas_call(
        _ffn_kernel,
        out_shape=jax.ShapeDtypeStruct((B, S, D_MODEL), F32),
        grid=(B, S // tm),
        in_specs=[pl.BlockSpec((1, tm, D_MODEL), row)] + _halo_specs(tm, S, D_MODEL) + [
            pl.BlockSpec((1, 1, 6 * D_MODEL), lambda b, i: (b, 0, 0)),
            _const_spec((1, D_MODEL)),
            _const_spec((D_MODEL, 2 * D_FF)),
            _const_spec((3, 2 * D_FF)),
            _const_spec((1, 2 * D_FF)),
            _const_spec((D_FF, D_MODEL)),
            _const_spec((1, D_MODEL)),
        ],
        out_specs=pl.BlockSpec((1, tm, D_MODEL), row),
        compiler_params=_params(("parallel", "parallel")),
        name="conv_ffn",
    )(x, x, x, mod, prep["norm2_g"], prep["w_up"], prep["conv_f_w"], prep["conv_f_b"],
      prep["w_down"], prep["final_g"])


def _head_groups(w, n_heads, width, pieces):
    rows = w.shape[0]
    wh = w.reshape(rows, n_heads, width)
    out = jnp.zeros((rows, n_heads, LANES), F32)
    for src, n, dst, sign in pieces:
        out = out.at[:, :, dst:dst + n].set(sign * wh[:, :, src:src + n])
    return out.reshape(rows, n_heads * LANES)


def _prepare(S, norm1_g, w_in, b_gate, q_norm_g, kv_norm_g, w_uq, w_ukv, conv_m_w, conv_m_b,
             mh_norm_g, w_out, norm2_g, w_up, conv_f_w, conv_f_b, w_down, final_g):
    half = D_ROPE // 2
    r1 = D_NOPE
    r2 = D_NOPE + half
    c1, c2 = Q_RANK + KV_RANK, Q_RANK + KV_RANK + D_ROPE
    c3 = c2 + 2 * D_MLSTM
    c4 = c3 + D_MLSTM
    c5 = c4 + D_MLSTM

    kr = w_in[:, c1:c2]
    zeros = lambda n: jnp.zeros((D_MODEL, n), F32)
    kr_main = jnp.concatenate([zeros(r1), kr[:, :half], kr[:, half:], zeros(LANES - r2 - half)], axis=1)
    kr_swap = jnp.concatenate([zeros(r1), -kr[:, half:], kr[:, :half], zeros(LANES - r2 - half)], axis=1)
    gates = jnp.concatenate([w_in[:, c5:], zeros(LANES - 4 * M_HEADS)], axis=1)
    w_prep = jnp.concatenate([w_in[:, :c1], kr_main, kr_swap, w_in[:, c2:c3], w_in[:, c4:c5], gates], axis=1)

    qs = (D_NOPE + D_ROPE) ** -0.5 * math.log2(math.e)
    dq = D_NOPE + D_ROPE
    wq_main = _head_groups(w_uq, N_HEADS, dq, [(0, D_NOPE, 0, 1.0), (D_NOPE, half, r1, 1.0),
                                                (D_NOPE + half, half, r2, 1.0)])
    wq_swap = _head_groups(w_uq, N_HEADS, dq, [(D_NOPE + half, half, r1, -1.0), (D_NOPE, half, r2, 1.0)])
    w_q = jnp.concatenate([wq_main, wq_swap], axis=1) * qs

    dkv = D_NOPE + D_V
    wk = _head_groups(w_ukv, N_HEADS, dkv, [(0, D_NOPE, 0, 1.0)])
    wv = _head_groups(w_ukv, N_HEADS, dkv, [(D_NOPE, D_V, 0, 1.0)])

    inv = 1.0 / (ROPE_THETA ** (jnp.arange(half, dtype=F32) * (2.0 / D_ROPE)))
    ang = jnp.arange(S, dtype=jnp.int32).astype(F32)[:, None] * inv[None, :]
    ones = jnp.ones((S, r1), F32)
    pad = jnp.zeros((S, LANES - r2 - half), F32)
    cos = jnp.concatenate([ones, jnp.cos(ang), jnp.cos(ang), pad], axis=1)
    sin = jnp.concatenate([0.0 * ones, jnp.sin(ang), jnp.sin(ang), pad], axis=1)

    v_one = jnp.zeros((N_HEADS, LANES), F32).at[:, D_V].set(1.0).reshape(N_HEADS * LANES, 1)
    b_g = jnp.concatenate([b_gate, jnp.zeros((LANES - 4 * M_HEADS,), F32)]).reshape(1, LANES)

    d_attn = N_HEADS * D_V
    w_oa = jnp.zeros((N_HEADS, LANES, D_MODEL), F32).at[:, :D_V, :].set(
        w_out[:d_attn].reshape(N_HEADS, D_V, D_MODEL)).reshape(N_HEADS * LANES, D_MODEL)

    return dict(
        norm1_g=norm1_g.reshape(1, -1), w_in=w_prep.astype(BF16), w_vm_t=w_in[:, c3:c4].T.astype(BF16), b_gate=b_g,
        q_norm_g=q_norm_g.reshape(1, -1), kv_norm_g=kv_norm_g.reshape(1, -1),
        w_q=w_q.astype(BF16), w_k=wk.astype(BF16), w_vt=wv.T.astype(BF16), cos=cos, sin=sin, v_one=v_one,
        conv_m_w=conv_m_w, conv_m_b=conv_m_b.reshape(1, -1), mh_norm_g=mh_norm_g.reshape(1, -1),
        w_out_attn=w_oa.astype(BF16), w_out_mem=w_out[d_attn:].astype(BF16),
        norm2_g=norm2_g.reshape(1, -1), w_up=w_up.astype(BF16), conv_f_w=conv_f_w,
        conv_f_b=conv_f_b.reshape(1, -1), w_down=w_down.astype(BF16), final_g=final_g.reshape(1, -1),
    )


def _tiles(S):
    tm = min(512, S // 2)
    return tm, min(512, S), 1


def _trunk(x, c, w_ada, b_ada, prep):
    S = x.shape[1]
    tm, tq, sub = _tiles(S)
    mod = _ada_mod(c, w_ada, b_ada)
    q, k, vt, qk_act, v_m, o_m, gates, stats = _in_proj(x, mod, prep, tm)
    attn = _attention(q, k, vt, stats, tq, sub)
    h_f, h_b = _mlstm(qk_act, v_m, gates)
    x1 = _out_proj(x, mod, attn, h_f, h_b, o_m, prep, tm)
    return _conv_ffn(x1, mod, prep, min(FFN_ROWS, tm))


def kernel(x_prompt, x_sample, c_prompt, c_sample, norm1_g, w_ada, b_ada, w_in, b_gate, q_norm_g, kv_norm_g, w_uq, w_ukv, conv_m_w, conv_m_b, mh_norm_g, w_out, norm2_g, w_up, conv_f_w, conv_f_b, w_down, final_g):
    assert w_ada.shape[0] == 1, "single-layer trunk"
    assert x_prompt.shape[1] == x_sample.shape[1]
    prep = _prepare(x_prompt.shape[1], norm1_g[0], w_in[0], b_gate[0], q_norm_g[0], kv_norm_g[0], w_uq[0],
                    w_ukv[0], conv_m_w[0], conv_m_b[0], mh_norm_g[0], w_out[0], norm2_g[0], w_up[0],
                    conv_f_w[0], conv_f_b[0], w_down[0], final_g)
    y_prompt = _trunk(x_prompt, c_prompt, w_ada[0], b_ada[0], prep)
    y_sample = _trunk(x_sample, c_sample, w_ada[0], b_ada[0], prep)
    return (y_prompt, y_sample)
```
